```python
import math
import jax, jax.numpy as jnp
from jax import lax
import numpy as np


D_MODEL = 1024
BATCH = 1
SEQ = 16384
DEPTH = 4

HEAD_DIM = 64
RET_HEADS = (3 * D_MODEL // 8) // HEAD_DIM
RET_WIDTH = RET_HEADS * HEAD_DIM
SWA_Q_HEADS = (3 * D_MODEL // 8) // HEAD_DIM
SWA_KV_HEADS = 2
SWA_GROUP = SWA_Q_HEADS // SWA_KV_HEADS
SWA_WIDTH = SWA_Q_HEADS * HEAD_DIM
SWA_KV_WIDTH = SWA_KV_HEADS * HEAD_DIM
CONV_CH = D_MODEL - RET_WIDTH - SWA_WIDTH
CONV_K = 3
MIX_WIDTH = RET_WIDTH + SWA_WIDTH + CONV_CH
IN_WIDTH = 4 * RET_WIDTH + SWA_WIDTH + 2 * SWA_KV_WIDTH + 3 * CONV_CH
FFN_HIDDEN = -(-8 * D_MODEL // (3 * 256)) * 256
WINDOW = 128
SWA_BLOCK = 128
RET_CHUNK = 128
ROPE_THETA = 10000.0
EPS = 1e-6
NEG_INF = -1e30

kernel_name = "hymba_style_retention_swa_shortconv_hybrid"


def _rms_norm(x, g):
    xf = x.astype(jnp.float32)
    y = xf * lax.rsqrt(jnp.mean(xf * xf, axis=-1, keepdims=True) + EPS)
    return (y * g).astype(x.dtype)


def _rope_tables(seq):
    inv = 1.0 / (ROPE_THETA ** (jnp.arange(0, HEAD_DIM, 2, dtype=jnp.float32) / HEAD_DIM))
    ang = jnp.arange(seq, dtype=jnp.float32)[:, None] * inv[None, :]
    return jnp.cos(ang)[:, None, :], jnp.sin(ang)[:, None, :]


def _rope(x, cos, sin):
    x1, x2 = jnp.split(x, 2, axis=-1)
    return jnp.concatenate([x1 * cos - x2 * sin, x2 * cos + x1 * sin], axis=-1)


def _retention(q, k, v, g, gn_gain, cos, sin):
    b, s, _ = q.shape
    n = s // RET_CHUNK
    q = _rope(q.reshape(b, s, RET_HEADS, HEAD_DIM), cos, sin)
    k = _rope(k.reshape(b, s, RET_HEADS, HEAD_DIM), cos, sin) * (HEAD_DIM ** -0.5)
    v = v.reshape(b, s, RET_HEADS, HEAD_DIM)
    qc = q.reshape(b, n, RET_CHUNK, RET_HEADS, HEAD_DIM)
    kc = k.reshape(b, n, RET_CHUNK, RET_HEADS, HEAD_DIM)
    vc = v.reshape(b, n, RET_CHUNK, RET_HEADS, HEAD_DIM)
    log_gamma = jnp.log(1.0 - 2.0 ** (-5.0 - jnp.arange(RET_HEADS, dtype=jnp.float32)))
    idx = jnp.arange(RET_CHUNK, dtype=jnp.float32)
    rel = idx[:, None] - idx[None, :]
    intra_decay = jnp.where(rel >= 0, jnp.exp(log_gamma[:, None, None] * jnp.maximum(rel, 0.0)), 0.0)
    scores = jnp.einsum('bnqhd,bnkhd->bnhqk', qc, kc) * intra_decay
    inner = jnp.einsum('bnhqk,bnkhe->bnqhe', scores, vc)
    k_decay = jnp.exp(log_gamma[:, None] * (RET_CHUNK - 1.0 - idx)[None, :])
    chunk_kv = jnp.einsum('bnkhd,hk,bnkhe->bnhde', kc, k_decay, vc).astype(jnp.float32)
    chunk_decay = jnp.exp(log_gamma * RET_CHUNK)[None, :, None, None]

    def step(state, kv_n):
        return state * chunk_decay + kv_n, state

    init = jnp.zeros((b, RET_HEADS, HEAD_DIM, HEAD_DIM), jnp.float32)
    _, prev = lax.scan(step, init, jnp.moveaxis(chunk_kv, 1, 0))
    prev = jnp.moveaxis(prev, 0, 1)
    q_decay = jnp.exp(log_gamma[:, None] * (idx + 1.0)[None, :])
    cross = jnp.einsum('bnqhd,bnhde,hq->bnqhe', qc, prev, q_decay)
    o = (inner + cross).astype(jnp.float32)
    mu = jnp.mean(o, axis=-1, keepdims=True)
    var = jnp.mean(jnp.square(o - mu), axis=-1, keepdims=True)
    o = ((o - mu) * lax.rsqrt(var + EPS)).reshape(b, s, RET_WIDTH) * gn_gain
    return (o * jax.nn.silu(g.astype(jnp.float32))).astype(g.dtype)


def _swa_sink_attention(q, k, v, q_norm, k_norm, sinks, cos, sin):
    b, s, _ = q.shape
    nb = s // SWA_BLOCK
    q = _rope(_rms_norm(q.reshape(b, s, SWA_Q_HEADS, HEAD_DIM), q_norm), cos, sin)
    k = _rope(_rms_norm(k.reshape(b, s, SWA_KV_HEADS, HEAD_DIM), k_norm), cos, sin)
    v = v.reshape(b, s, SWA_KV_HEADS, HEAD_DIM)
    qb = q.reshape(b, nb, SWA_BLOCK, SWA_KV_HEADS, SWA_GROUP, HEAD_DIM)
    kb = k.reshape(b, nb, SWA_BLOCK, SWA_KV_HEADS, HEAD_DIM)
    vb = v.reshape(b, nb, SWA_BLOCK, SWA_KV_HEADS, HEAD_DIM)
    kw = jnp.concatenate([jnp.concatenate([jnp.zeros_like(kb[:, :1]), kb[:, :-1]], axis=1), kb], axis=2)
    vw = jnp.concatenate([jnp.concatenate([jnp.zeros_like(vb[:, :1]), vb[:, :-1]], axis=1), vb], axis=2)
    sc = jnp.einsum('bnqhgd,bnkhd->bnhgqk', qb, kw).astype(jnp.float32) * (HEAD_DIM ** -0.5)
    qpos = jnp.arange(SWA_BLOCK) + SWA_BLOCK
    kpos = jnp.arange(2 * SWA_BLOCK)
    diff = qpos[:, None] - kpos[None, :]
    band = (diff >= 0) & (diff < WINDOW)
    first_ok = (jnp.arange(nb)[:, None, None] > 0) | (kpos[None, None, :] >= SWA_BLOCK)
    mask = band[None] & first_ok
    sc = jnp.where(mask[None, :, None, None], sc, NEG_INF)
    sink = sinks.astype(jnp.float32).reshape(SWA_KV_HEADS, SWA_GROUP)[None, None, :, :, None, None]
    m = jnp.maximum(jnp.max(sc, axis=-1, keepdims=True), sink)
    p = jnp.exp(sc - m)
    p = p / (jnp.sum(p, axis=-1, keepdims=True) + jnp.exp(sink - m))
    o = jnp.einsum('bnhgqk,bnkhd->bnqhgd', p.astype(vw.dtype), vw)
    return o.reshape(b, s, SWA_WIDTH)


def _short_conv(gb, gc, u, conv_w, conv_b):
    s = u.shape[1]
    z = gc * u
    zp = jnp.pad(z, ((0, 0), (CONV_K - 1, 0), (0, 0)))
    y = zp[:, 0:s] * conv_w[0] + zp[:, 1:s + 1] * conv_w[1] + zp[:, 2:s + 2] * conv_w[2] + conv_b
    return gb * y


def _layer(x, c, w_ada, b_ada, g_mix, w_in, conv_w, conv_b, q_norm, k_norm, sinks, ret_gn, w_out,
           g_ffn, w_gu, w_down, cos, sin):
    mod = jnp.einsum('bd,de->be', jax.nn.silu(c), w_ada) + b_ada
    sh1, sc1, ga1, sh2, sc2, ga2 = [m[:, None, :] for m in jnp.split(mod, 6, axis=-1)]
    h = _rms_norm(x, g_mix) * (1.0 + sc1) + sh1
    proj = jnp.einsum('bsd,de->bse', h, w_in)
    cuts = [RET_WIDTH, 2 * RET_WIDTH, 3 * RET_WIDTH, 4 * RET_WIDTH,
            4 * RET_WIDTH + SWA_WIDTH,
            4 * RET_WIDTH + SWA_WIDTH + SWA_KV_WIDTH,
            4 * RET_WIDTH + SWA_WIDTH + 2 * SWA_KV_WIDTH,
            4 * RET_WIDTH + SWA_WIDTH + 2 * SWA_KV_WIDTH + CONV_CH,
            4 * RET_WIDTH + SWA_WIDTH + 2 * SWA_KV_WIDTH + 2 * CONV_CH]
    rq, rk, rv, rg, aq, ak, av, cb, cc, cu = jnp.split(proj, cuts, axis=-1)
    o_ret = _retention(rq, rk, rv, rg, ret_gn, cos, sin)
    o_swa = _swa_sink_attention(aq, ak, av, q_norm, k_norm, sinks, cos, sin)
    o_conv = _short_conv(cb, cc, cu, conv_w, conv_b)
    mix = jnp.concatenate([o_ret, o_swa, o_conv], axis=-1)
    x = x + ga1 * jnp.einsum('bse,ed->bsd', mix, w_out)
    h = _rms_norm(x, g_ffn) * (1.0 + sc2) + sh2
    gt, up = jnp.split(jnp.einsum('bsd,df->bsf', h, w_gu), 2, axis=-1)
    x = x + ga2 * jnp.einsum('bsf,fd->bsd', jax.nn.silu(gt) * up, w_down)
    return x


def setup_inputs(seed: int = 0) -> dict:
    key = jax.random.key(seed)
    ks = jax.random.split(key, 18)

    def nrm(k, shape, scale):
        return jax.random.normal(k, shape, jnp.float32) * scale

    L, D = DEPTH, D_MODEL
    return {
        'x': nrm(ks[0], (BATCH, SEQ, D), 1.0),
        'c': nrm(ks[1], (BATCH, D), 1.0),
        'w_ada': nrm(ks[2], (L, D, 6 * D), 0.5 * D ** -0.5),
        'b_ada': nrm(ks[3], (L, 6 * D), 0.01),
        'g_mix': 1.0 + nrm(ks[4], (L, D), 0.02),
        'w_in': nrm(ks[5], (L, D, IN_WIDTH), D ** -0.5),
        'conv_w': nrm(ks[6], (L, CONV_K, CONV_CH), CONV_K ** -0.5),
        'conv_b': nrm(ks[7], (L, CONV_CH), 0.01),
        'q_norm': 1.0 + nrm(ks[8], (L, HEAD_DIM), 0.02),
        'k_norm': 1.0 + nrm(ks[9], (L, HEAD_DIM), 0.02),
        'sinks': nrm(ks[10], (L, SWA_Q_HEADS), 1.0),
        'ret_gn': 1.0 + nrm(ks[11], (L, RET_WIDTH), 0.02),
        'w_out': nrm(ks[12], (L, MIX_WIDTH, D), MIX_WIDTH ** -0.5),
        'g_ffn': 1.0 + nrm(ks[13], (L, D), 0.02),
        'w_gu': nrm(ks[14], (L, D, 2 * FFN_HIDDEN), D ** -0.5),
        'w_down': nrm(ks[15], (L, FFN_HIDDEN, D), FFN_HIDDEN ** -0.5),
    }


def reference(x, c, w_ada, b_ada, g_mix, w_in, conv_w, conv_b, q_norm, k_norm, sinks, ret_gn, w_out,
              g_ffn, w_gu, w_down):
    cos, sin = _rope_tables(x.shape[1])
    for l in range(DEPTH):
        x = _layer(x, c, w_ada[l], b_ada[l], g_mix[l], w_in[l], conv_w[l], conv_b[l], q_norm[l],
                   k_norm[l], sinks[l], ret_gn[l], w_out[l], g_ffn[l], w_gu[l], w_down[l], cos, sin)
    return x
```

```python
import functools

import jax
import jax.numpy as jnp
from jax import lax
from jax.experimental import pallas as pl
from jax.experimental.pallas import tpu as pltpu

HEAD_DIM = 64
LANES = 128
SWA_KV_HEADS = 2
CONV_K = 3
CHUNK = 128
ROPE_THETA = 10000.0
EPS = 1e-6
NEG_INF = -1e30
ROW_TILE = 512
FFN_COL = 256
ADA_COL = 1024
VMEM_LIMIT_BYTES = 56 * 1024 * 1024

F32 = jnp.float32
BF16 = jnp.bfloat16


def _dot(a, b):
    return jnp.dot(a, b, preferred_element_type=F32)


def _dot_nt(a, b):
    return lax.dot_general(a, b, (((1,), (1,)), ((), ())), preferred_element_type=F32)


def _dot_tn(a, b):
    return lax.dot_general(a, b, (((0,), (0,)), ((), ())), preferred_element_type=F32)


def _lane_lo(shape):
    return lax.broadcasted_iota(jnp.int32, shape, 1) % LANES < HEAD_DIM


def _head_sum(a, lo):
    s_lo = jnp.sum(jnp.where(lo, a, 0.0), axis=-1, keepdims=True)
    s_hi = jnp.sum(jnp.where(lo, 0.0, a), axis=-1, keepdims=True)
    return jnp.where(lo, s_lo, s_hi)


def _rope(a, cos, sin_signed, first_half):
    width = a.shape[-1]
    fwd = pltpu.roll(a, width - HEAD_DIM // 2, 1)
    bwd = pltpu.roll(a, HEAD_DIM // 2, 1)
    return a * cos + jnp.where(first_half, fwd, bwd) * sin_signed


def _silu(a):
    return a * jax.nn.sigmoid(a)


def _modulated_rms_norm(x, gain_scale, shift):
    ms = jnp.mean(x * x, axis=-1, keepdims=True)
    return (x * lax.rsqrt(ms + EPS) * gain_scale + shift).astype(BF16)


def _ada_kernel(c_ref, w_ref, b_ref, o_ref):
    s = _silu(c_ref[...])
    w = w_ref[0]
    s_hi = s.astype(BF16)
    s_lo = (s - s_hi.astype(F32)).astype(BF16)
    w_hi = w.astype(BF16)
    w_lo = (w - w_hi.astype(F32)).astype(BF16)
    acc = _dot(s_hi, w_lo) + _dot(s_lo, w_hi) + _dot(s_hi, w_hi)
    o_ref[0] = acc + b_ref[0]


def _ada_modulation(c, w_ada, b_ada):
    depth, d, e = w_ada.shape
    c8 = jnp.broadcast_to(c, (8, d))
    return pl.pallas_call(
        _ada_kernel,
        grid=(depth, e // ADA_COL),
        in_specs=[
            pl.BlockSpec((8, d), lambda l, j: (0, 0)),
            pl.BlockSpec((1, d, ADA_COL), lambda l, j: (l, 0, j)),
            pl.BlockSpec((1, 1, ADA_COL), lambda l, j: (l, 0, j)),
        ],
        out_specs=pl.BlockSpec((1, 8, ADA_COL), lambda l, j: (l, 0, j)),
        out_shape=jax.ShapeDtypeStruct((depth, 8, e), F32),
        compiler_params=pltpu.CompilerParams(
            dimension_semantics=("arbitrary", "arbitrary"), vmem_limit_bytes=VMEM_LIMIT_BYTES),
        name="ada_modulation",
    )(c8, w_ada, b_ada.reshape(depth, 1, e))


def _mixer_kernel(dims, sinks_ref, x_ref, mod_ref, gmix_ref, win_ref, cos_ref, sin_ref, convp_ref,
                  qn_ref, kn_ref, gn_ref, dpair_ref, qdec_ref, kdec_ref, sdec_ref, wout_ref,
                  o_ref, proj_s, mix_s, state_s, kprev_s, vprev_s, zbuf_s):
    ret_w, swa_w, kv_w, conv_c = dims
    n_pairs = ret_w // LANES
    n_qcols = swa_w // LANES
    group = (swa_w // HEAD_DIM) // SWA_KV_HEADS
    tile = x_ref.shape[0]
    step = pl.program_id(0)

    off_rq, off_rk, off_rv, off_rg = 0, ret_w, 2 * ret_w, 3 * ret_w
    off_aq = 4 * ret_w
    off_ak = off_aq + swa_w
    off_av = off_ak + kv_w
    off_cb = off_av + kv_w
    off_cc = off_cb + conv_c
    off_cu = off_cc + conv_c

    @pl.when(step == 0)
    def _init():
        state_s[...] = jnp.zeros_like(state_s)
        kprev_s[...] = jnp.zeros_like(kprev_s)
        vprev_s[...] = jnp.zeros_like(vprev_s)
        zbuf_s[0:8, :] = jnp.zeros((8, conv_c), F32)

    shift1, scale1, gate1 = mod_ref[0], mod_ref[1], mod_ref[2]
    x = x_ref[...]
    h = _modulated_rms_norm(x, gmix_ref[...] * (1.0 + scale1), shift1)
    proj_s[...] = _dot(h, win_ref[...])

    sq = (CHUNK, LANES)
    lo = _lane_lo(sq)
    first_half = lax.broadcasted_iota(jnp.int32, sq, 1) % HEAD_DIM < HEAD_DIM // 2
    row = lax.broadcasted_iota(jnp.int32, sq, 0)
    col = lax.broadcasted_iota(jnp.int32, sq, 1)
    block_diag = (row < HEAD_DIM) == (col < HEAD_DIM)
    qi = lax.broadcasted_iota(jnp.int32, (CHUNK, 2 * CHUNK), 0)
    kj = lax.broadcasted_iota(jnp.int32, (CHUNK, 2 * CHUNK), 1)
    band = (kj > qi) & (kj <= qi + CHUNK)
    zero_b = jnp.zeros(sq, BF16)
    ones_b = jnp.ones((2 * CHUNK, LANES), BF16)

    def blk(off, c, width=LANES):
        return (pl.ds(c * CHUNK, CHUNK), pl.ds(off, width))

    states = [state_s[p] for p in range(n_pairs)]
    k_prev = [kprev_s[grp] for grp in range(SWA_KV_HEADS)]
    v_prev = [vprev_s[grp] for grp in range(SWA_KV_HEADS)]

    for c in range(tile // CHUNK):
        rows = pl.ds(c * CHUNK, CHUNK)
        cos = cos_ref[rows, :]
        sin = sin_ref[rows, :]

        for p in range(n_pairs):
            q = _rope(proj_s[blk(off_rq + p * LANES, c)], cos, sin, first_half)
            k = _rope(proj_s[blk(off_rk + p * LANES, c)], cos, sin, first_half) * (HEAD_DIM ** -0.5)
            v = proj_s[blk(off_rv + p * LANES, c)].astype(BF16)
            g = proj_s[blk(off_rg + p * LANES, c)]
            kb = k.astype(BF16)
            k_bd = jnp.concatenate([jnp.where(lo, kb, zero_b), jnp.where(lo, zero_b, kb)], axis=0)
            v_bd = jnp.concatenate([jnp.where(lo, v, zero_b), jnp.where(lo, zero_b, v)], axis=0)
            scores = _dot_nt(q.astype(BF16), k_bd) * dpair_ref[p]
            inner = _dot(scores.astype(BF16), v_bd)
            cross = _dot((q * qdec_ref[p]).astype(BF16), states[p].astype(BF16))
            kv = _dot_tn((k * kdec_ref[p]).astype(BF16), v)
            states[p] = states[p] * sdec_ref[p] + jnp.where(block_diag, kv, 0.0)
            o = inner + cross
            mu = _head_sum(o, lo) * (1.0 / HEAD_DIM)
            d = o - mu
            var = _head_sum(d * d, lo) * (1.0 / HEAD_DIM)
            o = d * lax.rsqrt(var + EPS) * gn_ref[:, pl.ds(p * LANES, LANES)]
            mix_s[blk(p * LANES, c)] = (o * _silu(g)).astype(BF16)

        def head_norm(a, gain):
            ms = _head_sum(a * a, lo) * (1.0 / HEAD_DIM)
            return a * lax.rsqrt(ms + EPS) * gain

        k = _rope(head_norm(proj_s[blk(off_ak, c)], kn_ref[...]), cos, sin, first_half)
        v = proj_s[blk(off_av, c)]
        k_sw = pltpu.roll(k, HEAD_DIM, 1)
        v_sw = pltpu.roll(v, HEAD_DIM, 1)
        k_cur = [jnp.where(lo, k, k_sw).astype(BF16), jnp.where(lo, k_sw, k).astype(BF16)]
        v_cur = [jnp.where(lo, v, v_sw).astype(BF16), jnp.where(lo, v_sw, v).astype(BF16)]
        qcols = [
            _rope(head_norm(proj_s[blk(off_aq + j * LANES, c)], qn_ref[...]), cos, sin, first_half)
            * (HEAD_DIM ** -0.5)
            for j in range(n_qcols)
        ]
        if c == 0:
            first_key = jnp.where(step > 0, 0, CHUNK)
            mask = band & (kj >= first_key)
        else:
            mask = band
        heads_out = []
        for grp in range(SWA_KV_HEADS):
            heads = [grp * group + j for j in range(group)]
            q_st = jnp.concatenate(
                [jnp.where(lo if hd % 2 == 0 else ~lo, qcols[hd // 2], 0.0).astype(BF16) for hd in heads],
                axis=0)
            k_win = jnp.concatenate([k_prev[grp], k_cur[grp]], axis=0)
            v_win = jnp.concatenate(
                [jnp.concatenate([v_prev[grp], v_cur[grp]], axis=0), ones_b], axis=1)
            sc = _dot_nt(q_st, k_win)
            probs, maxes = [], []
            for j, hd in enumerate(heads):
                s_h = jnp.where(mask, sc[j * CHUNK:(j + 1) * CHUNK], NEG_INF)
                m = jnp.maximum(jnp.max(s_h, axis=-1, keepdims=True), sinks_ref[hd])
                probs.append(jnp.exp(s_h - m).astype(BF16))
                maxes.append(m)
            ov = _dot(jnp.concatenate(probs, axis=0), v_win)
            for j, hd in enumerate(heads):
                o_h = ov[j * CHUNK:(j + 1) * CHUNK]
                denom = o_h[:, LANES:] + jnp.exp(sinks_ref[hd] - maxes[j])
                heads_out.append(o_h[:, :LANES] / denom)
        for j in range(n_qcols):
            mix_s[blk(ret_w + j * LANES, c)] = jnp.where(lo, heads_out[2 * j], heads_out[2 * j + 1]).astype(BF16)
        k_prev, v_prev = k_cur, v_cur

        z = proj_s[blk(off_cc, c, conv_c)] * proj_s[blk(off_cu, c, conv_c)]
        zbuf_s[pl.ds(8 + c * CHUNK, CHUNK), :] = z
        z1 = zbuf_s[pl.ds(7 + c * CHUNK, CHUNK), :]
        z2 = zbuf_s[pl.ds(6 + c * CHUNK, CHUNK), :]
        y = z2 * convp_ref[0:1, :] + z1 * convp_ref[1:2, :] + z * convp_ref[2:3, :] + convp_ref[3:4, :]
        mix_s[blk(ret_w + swa_w, c, conv_c)] = (proj_s[blk(off_cb, c, conv_c)] * y).astype(BF16)

    for p in range(n_pairs):
        state_s[p] = states[p]
    for grp in range(SWA_KV_HEADS):
        kprev_s[grp] = k_prev[grp]
        vprev_s[grp] = v_prev[grp]
    zbuf_s[0:8, :] = zbuf_s[tile:tile + 8, :]
    o_ref[...] = x + gate1 * _dot(mix_s[...], wout_ref[...])


def _resident(shape):
    zeros = (0,) * len(shape)
    return pl.BlockSpec(shape, lambda i: zeros, pipeline_mode=pl.Buffered(1))


def _mixer(x, sinks, mod, g_mix, w_in, cos_t, sin_t, convp, qn, kn, gn, tables, w_out, dims):
    seq, d = x.shape
    ret_w, swa_w, kv_w, conv_c = dims
    in_w = w_in.shape[1]
    n_pairs = ret_w // LANES
    dpair, qdec, kdec, sdec = tables
    tile = ROW_TILE
    return pl.pallas_call(
        functools.partial(_mixer_kernel, dims),
        grid=(seq // tile,),
        in_specs=[
            pl.BlockSpec(memory_space=pltpu.SMEM),
            pl.BlockSpec((tile, d), lambda i: (i, 0)),
            _resident(mod.shape),
            _resident(g_mix.shape),
            _resident(w_in.shape),
            pl.BlockSpec((tile, LANES), lambda i: (i, 0)),
            pl.BlockSpec((tile, LANES), lambda i: (i, 0)),
            _resident(convp.shape),
            _resident(qn.shape),
            _resident(kn.shape),
            _resident(gn.shape),
            _resident(dpair.shape),
            _resident(qdec.shape),
            _resident(kdec.shape),
            _resident(sdec.shape),
            _resident(w_out.shape),
        ],
        out_specs=pl.BlockSpec((tile, d), lambda i: (i, 0)),
        out_shape=jax.ShapeDtypeStruct((seq, d), F32),
        scratch_shapes=[
            pltpu.VMEM((tile, in_w), F32),
            pltpu.VMEM((tile, d), BF16),
            pltpu.VMEM((n_pairs, LANES, LANES), F32),
            pltpu.VMEM((SWA_KV_HEADS, CHUNK, LANES), BF16),
            pltpu.VMEM((SWA_KV_HEADS, CHUNK, LANES), BF16),
            pltpu.VMEM((tile + 8, conv_c), F32),
        ],
        compiler_params=pltpu.CompilerParams(
            dimension_semantics=("arbitrary",), vmem_limit_bytes=VMEM_LIMIT_BYTES),
        name="mixer",
    )(sinks, x, mod, g_mix, w_in, cos_t, sin_t, convp, qn, kn, gn, dpair, qdec, kdec, sdec, w_out)


def _ffn_kernel(x_ref, mod_ref, gffn_ref, wgu_ref, wdown_ref, o_ref, act_s):
    hidden = wdown_ref.shape[0]
    shift2, scale2, gate2 = mod_ref[0], mod_ref[1], mod_ref[2]
    x = x_ref[...]
    h = _modulated_rms_norm(x, gffn_ref[...] * (1.0 + scale2), shift2)
    for j in range(hidden // FFN_COL):
        cols = pl.ds(j * FFN_COL, FFN_COL)
        gt = _dot(h, wgu_ref[:, cols])
        up = _dot(h, wgu_ref[:, pl.ds(hidden + j * FFN_COL, FFN_COL)])
        act_s[:, cols] = (_silu(gt) * up).astype(BF16)
    o_ref[...] = x + gate2 * _dot(act_s[...], wdown_ref[...])


def _ffn(x, mod, g_ffn, w_gu, w_down):
    seq, d = x.shape
    hidden = w_down.shape[0]
    tile = ROW_TILE
    return pl.pallas_call(
        _ffn_kernel,
        grid=(seq // tile,),
        in_specs=[
            pl.BlockSpec((tile, d), lambda i: (i, 0)),
            _resident(mod.shape),
            _resident(g_ffn.shape),
            _resident(w_gu.shape),
            _resident(w_down.shape),
        ],
        out_specs=pl.BlockSpec((tile, d), lambda i: (i, 0)),
        out_shape=jax.ShapeDtypeStruct((seq, d), F32),
        scratch_shapes=[pltpu.VMEM((tile, hidden), BF16)],
        compiler_params=pltpu.CompilerParams(
            dimension_semantics=("arbitrary",), vmem_limit_bytes=VMEM_LIMIT_BYTES),
        name="ffn",
    )(x, mod, g_ffn, w_gu, w_down)


def _rope_tables(seq):
    inv = 1.0 / (ROPE_THETA ** (jnp.arange(0, HEAD_DIM, 2, dtype=F32) / HEAD_DIM))
    ang = jnp.arange(seq, dtype=F32)[:, None] * inv[None, :]
    cos, sin = jnp.cos(ang), jnp.sin(ang)
    reps = LANES // (HEAD_DIM // 2)
    cos_t = jnp.tile(cos, (1, reps))
    sin_t = jnp.tile(jnp.concatenate([-sin, sin], axis=-1), (1, reps // 2))
    return cos_t, sin_t


def _retention_tables(ret_heads):
    log_gamma = jnp.log(1.0 - 2.0 ** (-5.0 - jnp.arange(ret_heads, dtype=F32)))
    idx = jnp.arange(CHUNK, dtype=F32)
    rel = idx[:, None] - idx[None, :]
    intra = jnp.where(rel >= 0, jnp.exp(log_gamma[:, None, None] * jnp.maximum(rel, 0.0)), 0.0)
    k_decay = jnp.exp(log_gamma[:, None] * (CHUNK - 1.0 - idx)[None, :])
    q_decay = jnp.exp(log_gamma[:, None] * (idx + 1.0)[None, :])
    chunk_decay = jnp.exp(log_gamma * CHUNK)
    n_pairs = ret_heads // 2

    def lanes(per_head):
        wide = jnp.repeat(per_head[:, :, None], HEAD_DIM, axis=2)
        return jnp.concatenate([wide[0::2], wide[1::2]], axis=2)

    dpair = jnp.concatenate([intra[0::2], intra[1::2]], axis=2)
    sdec = jnp.broadcast_to(
        jnp.repeat(chunk_decay.reshape(n_pairs, 2), HEAD_DIM, axis=1)[:, :, None], (n_pairs, LANES, LANES))
    return dpair, lanes(q_decay), lanes(k_decay), sdec


@jax.jit
def kernel(x, c, w_ada, b_ada, g_mix, w_in, conv_w, conv_b, q_norm, k_norm, sinks, ret_gn, w_out, g_ffn,
           w_gu, w_down):
    batch, seq, d = x.shape
    depth = w_ada.shape[0]
    assert batch == 1 and seq % ROW_TILE == 0
    ret_w = ret_gn.shape[1]
    swa_w = sinks.shape[1] * HEAD_DIM
    conv_c = conv_w.shape[2]
    kv_w = SWA_KV_HEADS * HEAD_DIM
    assert w_in.shape[2] == 4 * ret_w + swa_w + 2 * kv_w + 3 * conv_c
    assert ret_w % LANES == 0 and swa_w % LANES == 0 and kv_w == LANES and conv_c % LANES == 0
    assert w_down.shape[1] % FFN_COL == 0 and w_ada.shape[2] % ADA_COL == 0
    dims = (ret_w, swa_w, kv_w, conv_c)

    mod = _ada_modulation(c, w_ada, b_ada)[:, 0, :].reshape(depth, 6, 1, d)
    cos_t, sin_t = _rope_tables(seq)
    tables = _retention_tables(ret_w // HEAD_DIM)
    reps = LANES // HEAD_DIM
    w_in_b, w_out_b = w_in.astype(BF16), w_out.astype(BF16)
    w_gu_b, w_down_b = w_gu.astype(BF16), w_down.astype(BF16)

    xs = x[0]
    for l in range(depth):
        convp = jnp.concatenate([conv_w[l], conv_b[l][None, :]], axis=0)
        xs = _mixer(xs, sinks[l], mod[l, :3], g_mix[l][None, :], w_in_b[l], cos_t, sin_t, convp,
                    jnp.tile(q_norm[l], reps)[None, :], jnp.tile(k_norm[l], reps)[None, :],
                    ret_gn[l][None, :], tables, w_out_b[l], dims)
        xs = _ffn(xs, mod[l, 3:], g_ffn[l][None, :], w_gu_b[l], w_down_b[l])
    return xs[None]
```

```python
import functools

import jax
import jax.numpy as jnp
from jax import lax
from jax.experimental import pallas as pl
from jax.experimental.pallas import tpu as pltpu

HEAD_DIM = 64
LANES = 128
SWA_KV_HEADS = 2
CONV_K = 3
CHUNK = 128
ROPE_THETA = 10000.0
EPS = 1e-6
NEG_INF = -1e30
ROW_TILE = 512
FFN_COL = 256
ADA_COL = 1024
VMEM_LIMIT_BYTES = 56 * 1024 * 1024

F32 = jnp.float32
BF16 = jnp.bfloat16


def _dot(a, b):
    return jnp.dot(a, b, preferred_element_type=F32)


def _dot_nt(a, b):
    return lax.dot_general(a, b, (((1,), (1,)), ((), ())), preferred_element_type=F32)


def _dot_tn(a, b):
    return lax.dot_general(a, b, (((0,), (0,)), ((), ())), preferred_element_type=F32)


def _lane_lo(shape):
    return lax.broadcasted_iota(jnp.int32, shape, 1) % LANES < HEAD_DIM


def _head_sum(a, lo):
    s_lo = jnp.sum(jnp.where(lo, a, 0.0), axis=-1, keepdims=True)
    s_hi = jnp.sum(jnp.where(lo, 0.0, a), axis=-1, keepdims=True)
    return jnp.where(lo, s_lo, s_hi)


def _rope(a, cos, sin_signed, first_half):
    width = a.shape[-1]
    fwd = pltpu.roll(a, width - HEAD_DIM // 2, 1)
    bwd = pltpu.roll(a, HEAD_DIM // 2, 1)
    return a * cos + jnp.where(first_half, fwd, bwd) * sin_signed


def _silu(a):
    return a * jax.nn.sigmoid(a)


def _modulated_rms_norm(x, gain_scale, shift):
    ms = jnp.mean(x * x, axis=-1, keepdims=True)
    return (x * lax.rsqrt(ms + EPS) * gain_scale + shift).astype(BF16)


def _ada_kernel(c_ref, w_ref, b_ref, o_ref):
    s = _silu(c_ref[...])
    w = w_ref[0]
    s_hi = s.astype(BF16)
    s_lo = (s - s_hi.astype(F32)).astype(BF16)
    w_hi = w.astype(BF16)
    w_lo = (w - w_hi.astype(F32)).astype(BF16)
    acc = _dot(s_hi, w_lo) + _dot(s_lo, w_hi) + _dot(s_hi, w_hi)
    o_ref[0] = acc + b_ref[0]


def _ada_modulation(c, w_ada, b_ada):
    depth, d, e = w_ada.shape
    c8 = jnp.broadcast_to(c, (8, d))
    return pl.pallas_call(
        _ada_kernel,
        grid=(depth, e // ADA_COL),
        in_specs=[
            pl.BlockSpec((8, d), lambda l, j: (0, 0)),
            pl.BlockSpec((1, d, ADA_COL), lambda l, j: (l, 0, j)),
            pl.BlockSpec((1, 1, ADA_COL), lambda l, j: (l, 0, j)),
        ],
        out_specs=pl.BlockSpec((1, 8, ADA_COL), lambda l, j: (l, 0, j)),
        out_shape=jax.ShapeDtypeStruct((depth, 8, e), F32),
        compiler_params=pltpu.CompilerParams(
            dimension_semantics=("arbitrary", "arbitrary"), vmem_limit_bytes=VMEM_LIMIT_BYTES),
        name="ada_modulation",
    )(c8, w_ada, b_ada.reshape(depth, 1, e))


def _mixer_kernel(dims, layer, sinks_ref, x_ref, mod_ref, gmix_ref, win_ref, cos_ref, sin_ref, convp_ref,
                  qn_ref, kn_ref, gn_ref, dpair_ref, qdec_ref, kdec_ref, sdec_ref, wout_ref,
                  o_ref, proj_s, mix_s, state_s, kprev_s, vprev_s, zbuf_s):
    ret_w, swa_w, kv_w, conv_c = dims
    n_pairs = ret_w // LANES
    n_qcols = swa_w // LANES
    group = (swa_w // HEAD_DIM) // SWA_KV_HEADS
    tile = x_ref.shape[0]
    step = pl.program_id(0)

    off_rq, off_rk, off_rv, off_rg = 0, ret_w, 2 * ret_w, 3 * ret_w
    off_aq = 4 * ret_w
    off_ak = off_aq + swa_w
    off_av = off_ak + kv_w
    off_cb = off_av + kv_w
    off_cc = off_cb + conv_c
    off_cu = off_cc + conv_c

    @pl.when(step == 0)
    def _init():
        state_s[...] = jnp.zeros_like(state_s)
        kprev_s[...] = jnp.zeros_like(kprev_s)
        vprev_s[...] = jnp.zeros_like(vprev_s)
        zbuf_s[0:8, :] = jnp.zeros((8, conv_c), F32)

    shift1, scale1, gate1 = mod_ref[0, 0], mod_ref[0, 1], mod_ref[0, 2]
    x = x_ref[...]
    h = _modulated_rms_norm(x, gmix_ref[0] * (1.0 + scale1), shift1)
    proj_s[...] = _dot(h, win_ref[0])

    sq = (CHUNK, LANES)
    lo = _lane_lo(sq)
    first_half = lax.broadcasted_iota(jnp.int32, sq, 1) % HEAD_DIM < HEAD_DIM // 2
    row = lax.broadcasted_iota(jnp.int32, sq, 0)
    col = lax.broadcasted_iota(jnp.int32, sq, 1)
    block_diag = (row < HEAD_DIM) == (col < HEAD_DIM)
    qi = lax.broadcasted_iota(jnp.int32, (CHUNK, 2 * CHUNK), 0)
    kj = lax.broadcasted_iota(jnp.int32, (CHUNK, 2 * CHUNK), 1)
    band = (kj > qi) & (kj <= qi + CHUNK)
    zero_b = jnp.zeros(sq, BF16)
    ones_b = jnp.ones((2 * CHUNK, LANES), BF16)

    def blk(off, c, width=LANES):
        return (pl.ds(c * CHUNK, CHUNK), pl.ds(off, width))

    states = [state_s[p] for p in range(n_pairs)]
    k_prev = [kprev_s[grp] for grp in range(SWA_KV_HEADS)]
    v_prev = [vprev_s[grp] for grp in range(SWA_KV_HEADS)]

    for c in range(tile // CHUNK):
        rows = pl.ds(c * CHUNK, CHUNK)
        cos = cos_ref[rows, :]
        sin = sin_ref[rows, :]

        for p in range(n_pairs):
            q = _rope(proj_s[blk(off_rq + p * LANES, c)], cos, sin, first_half)
            k = _rope(proj_s[blk(off_rk + p * LANES, c)], cos, sin, first_half) * (HEAD_DIM ** -0.5)
            v = proj_s[blk(off_rv + p * LANES, c)].astype(BF16)
            g = proj_s[blk(off_rg + p * LANES, c)]
            kb = k.astype(BF16)
            k_bd = jnp.concatenate([jnp.where(lo, kb, zero_b), jnp.where(lo, zero_b, kb)], axis=0)
            v_bd = jnp.concatenate([jnp.where(lo, v, zero_b), jnp.where(lo, zero_b, v)], axis=0)
            scores = _dot_nt(q.astype(BF16), k_bd) * dpair_ref[p]
            inner = _dot(scores.astype(BF16), v_bd)
            cross = _dot((q * qdec_ref[p]).astype(BF16), states[p].astype(BF16))
            kv = _dot_tn((k * kdec_ref[p]).astype(BF16), v)
            states[p] = states[p] * sdec_ref[p] + jnp.where(block_diag, kv, 0.0)
            o = inner + cross
            mu = _head_sum(o, lo) * (1.0 / HEAD_DIM)
            d = o - mu
            var = _head_sum(d * d, lo) * (1.0 / HEAD_DIM)
            o = d * lax.rsqrt(var + EPS) * gn_ref[0, :, pl.ds(p * LANES, LANES)]
            mix_s[blk(p * LANES, c)] = (o * _silu(g)).astype(BF16)

        def head_norm(a, gain):
            ms = _head_sum(a * a, lo) * (1.0 / HEAD_DIM)
            return a * lax.rsqrt(ms + EPS) * gain

        k = _rope(head_norm(proj_s[blk(off_ak, c)], kn_ref[0]), cos, sin, first_half)
        v = proj_s[blk(off_av, c)]
        k_sw = pltpu.roll(k, HEAD_DIM, 1)
        v_sw = pltpu.roll(v, HEAD_DIM, 1)
        k_cur = [jnp.where(lo, k, k_sw).astype(BF16), jnp.where(lo, k_sw, k).astype(BF16)]
        v_cur = [jnp.where(lo, v, v_sw).astype(BF16), jnp.where(lo, v_sw, v).astype(BF16)]
        qcols = [
            _rope(head_norm(proj_s[blk(off_aq + j * LANES, c)], qn_ref[0]), cos, sin, first_half)
            * (HEAD_DIM ** -0.5)
            for j in range(n_qcols)
        ]
        if c == 0:
            first_key = jnp.where(step > 0, 0, CHUNK)
            mask = band & (kj >= first_key)
        else:
            mask = band
        heads_out = []
        for grp in range(SWA_KV_HEADS):
            heads = [grp * group + j for j in range(group)]
            q_st = jnp.concatenate(
                [jnp.where(lo if hd % 2 == 0 else ~lo, qcols[hd // 2], 0.0).astype(BF16) for hd in heads],
                axis=0)
            k_win = jnp.concatenate([k_prev[grp], k_cur[grp]], axis=0)
            v_win = jnp.concatenate(
                [jnp.concatenate([v_prev[grp], v_cur[grp]], axis=0), ones_b], axis=1)
            sc = _dot_nt(q_st, k_win)
            probs, maxes = [], []
            for j, hd in enumerate(heads):
                s_h = jnp.where(mask, sc[j * CHUNK:(j + 1) * CHUNK], NEG_INF)
                m = jnp.maximum(jnp.max(s_h, axis=-1, keepdims=True), sinks_ref[layer, hd])
                probs.append(jnp.exp(s_h - m).astype(BF16))
                maxes.append(m)
            ov = _dot(jnp.concatenate(probs, axis=0), v_win)
            for j, hd in enumerate(heads):
                o_h = ov[j * CHUNK:(j + 1) * CHUNK]
                denom = o_h[:, LANES:] + jnp.exp(sinks_ref[layer, hd] - maxes[j])
                heads_out.append(o_h[:, :LANES] / denom)
        for j in range(n_qcols):
            mix_s[blk(ret_w + j * LANES, c)] = jnp.where(lo, heads_out[2 * j], heads_out[2 * j + 1]).astype(BF16)
        k_prev, v_prev = k_cur, v_cur

        z = proj_s[blk(off_cc, c, conv_c)] * proj_s[blk(off_cu, c, conv_c)]
        zbuf_s[pl.ds(8 + c * CHUNK, CHUNK), :] = z
        z1 = zbuf_s[pl.ds(7 + c * CHUNK, CHUNK), :]
        z2 = zbuf_s[pl.ds(6 + c * CHUNK, CHUNK), :]
        y = (z2 * convp_ref[0, 0:1, :] + z1 * convp_ref[0, 1:2, :] + z * convp_ref[0, 2:3, :]
             + convp_ref[0, 3:4, :])
        mix_s[blk(ret_w + swa_w, c, conv_c)] = (proj_s[blk(off_cb, c, conv_c)] * y).astype(BF16)

    for p in range(n_pairs):
        state_s[p] = states[p]
    for grp in range(SWA_KV_HEADS):
        kprev_s[grp] = k_prev[grp]
        vprev_s[grp] = v_prev[grp]
    zbuf_s[0:8, :] = zbuf_s[tile:tile + 8, :]
    o_ref[...] = x + gate1 * _dot(mix_s[...], wout_ref[0])


def _fixed_block(block, index):
    return pl.BlockSpec(block, lambda i: index, pipeline_mode=pl.Buffered(1))


def _whole(arr):
    return _fixed_block(arr.shape, (0,) * arr.ndim)


def _layer_slice(arr, layer):
    return _fixed_block((1,) + arr.shape[1:], (layer,) + (0,) * (arr.ndim - 1))


def _mixer(layer, x, sinks, mod, g_mix, w_in, cos_t, sin_t, convp, qn, kn, gn, tables, w_out, dims):
    seq, d = x.shape
    ret_w, swa_w, kv_w, conv_c = dims
    in_w = w_in.shape[2]
    n_pairs = ret_w // LANES
    dpair, qdec, kdec, sdec = tables
    tile = ROW_TILE
    return pl.pallas_call(
        functools.partial(_mixer_kernel, dims, layer),
        grid=(seq // tile,),
        in_specs=[
            pl.BlockSpec(memory_space=pltpu.SMEM),
            pl.BlockSpec((tile, d), lambda i: (i, 0)),
            _fixed_block((1, 3) + mod.shape[2:], (layer, 0, 0, 0)),
            _layer_slice(g_mix, layer),
            _layer_slice(w_in, layer),
            pl.BlockSpec((tile, LANES), lambda i: (i, 0)),
            pl.BlockSpec((tile, LANES), lambda i: (i, 0)),
            _layer_slice(convp, layer),
            _layer_slice(qn, layer),
            _layer_slice(kn, layer),
            _layer_slice(gn, layer),
            _whole(dpair),
            _whole(qdec),
            _whole(kdec),
            _whole(sdec),
            _layer_slice(w_out, layer),
        ],
        out_specs=pl.BlockSpec((tile, d), lambda i: (i, 0)),
        out_shape=jax.ShapeDtypeStruct((seq, d), F32),
        scratch_shapes=[
            pltpu.VMEM((tile, in_w), F32),
            pltpu.VMEM((tile, d), BF16),
            pltpu.VMEM((n_pairs, LANES, LANES), F32),
            pltpu.VMEM((SWA_KV_HEADS, CHUNK, LANES), BF16),
            pltpu.VMEM((SWA_KV_HEADS, CHUNK, LANES), BF16),
            pltpu.VMEM((tile + 8, conv_c), F32),
        ],
        compiler_params=pltpu.CompilerParams(
            dimension_semantics=("arbitrary",), vmem_limit_bytes=VMEM_LIMIT_BYTES),
        name="mixer",
    )(sinks, x, mod, g_mix, w_in, cos_t, sin_t, convp, qn, kn, gn, dpair, qdec, kdec, sdec, w_out)


def _ffn_kernel(x_ref, mod_ref, gffn_ref, wgu_ref, wdown_ref, o_ref, act_s):
    hidden = wdown_ref.shape[1]
    shift2, scale2, gate2 = mod_ref[0, 0], mod_ref[0, 1], mod_ref[0, 2]
    x = x_ref[...]
    h = _modulated_rms_norm(x, gffn_ref[0] * (1.0 + scale2), shift2)
    for j in range(hidden // FFN_COL):
        cols = pl.ds(j * FFN_COL, FFN_COL)
        gt = _dot(h, wgu_ref[0, :, cols])
        up = _dot(h, wgu_ref[0, :, pl.ds(hidden + j * FFN_COL, FFN_COL)])
        act_s[:, cols] = (_silu(gt) * up).astype(BF16)
    o_ref[...] = x + gate2 * _dot(act_s[...], wdown_ref[0])


def _ffn(layer, x, mod, g_ffn, w_gu, w_down):
    seq, d = x.shape
    hidden = w_down.shape[1]
    tile = ROW_TILE
    return pl.pallas_call(
        _ffn_kernel,
        grid=(seq // tile,),
        in_specs=[
            pl.BlockSpec((tile, d), lambda i: (i, 0)),
            _fixed_block((1, 3) + mod.shape[2:], (layer, 1, 0, 0)),
            _layer_slice(g_ffn, layer),
            _layer_slice(w_gu, layer),
            _layer_slice(w_down, layer),
        ],
        out_specs=pl.BlockSpec((tile, d), lambda i: (i, 0)),
        out_shape=jax.ShapeDtypeStruct((seq, d), F32),
        scratch_shapes=[pltpu.VMEM((tile, hidden), BF16)],
        compiler_params=pltpu.CompilerParams(
            dimension_semantics=("arbitrary",), vmem_limit_bytes=VMEM_LIMIT_BYTES),
        name="ffn",
    )(x, mod, g_ffn, w_gu, w_down)


def _rope_tables(seq):
    inv = 1.0 / (ROPE_THETA ** (jnp.arange(0, HEAD_DIM, 2, dtype=F32) / HEAD_DIM))
    half = HEAD_DIM // 2
    lane = jnp.arange(LANES)
    ang = jnp.arange(seq, dtype=F32)[:, None] * inv[lane % half][None, :]
    sign = jnp.where(lane % HEAD_DIM < half, -1.0, 1.0).astype(F32)
    return jnp.cos(ang), jnp.sin(ang) * sign[None, :]


def _retention_tables(ret_heads):
    log_gamma = jnp.log(1.0 - 2.0 ** (-5.0 - jnp.arange(ret_heads, dtype=F32)))
    idx = jnp.arange(CHUNK, dtype=F32)
    rel = idx[:, None] - idx[None, :]
    intra = jnp.where(rel >= 0, jnp.exp(log_gamma[:, None, None] * jnp.maximum(rel, 0.0)), 0.0)
    k_decay = jnp.exp(log_gamma[:, None] * (CHUNK - 1.0 - idx)[None, :])
    q_decay = jnp.exp(log_gamma[:, None] * (idx + 1.0)[None, :])
    chunk_decay = jnp.exp(log_gamma * CHUNK)
    n_pairs = ret_heads // 2

    def lanes(per_head):
        wide = jnp.repeat(per_head[:, :, None], HEAD_DIM, axis=2)
        return jnp.concatenate([wide[0::2], wide[1::2]], axis=2)

    dpair = jnp.concatenate([intra[0::2], intra[1::2]], axis=2)
    sdec = jnp.broadcast_to(
        jnp.repeat(chunk_decay.reshape(n_pairs, 2), HEAD_DIM, axis=1)[:, :, None], (n_pairs, LANES, LANES))
    return dpair, lanes(q_decay), lanes(k_decay), sdec


@jax.jit
def kernel(x, c, w_ada, b_ada, g_mix, w_in, conv_w, conv_b, q_norm, k_norm, sinks, ret_gn, w_out, g_ffn,
           w_gu, w_down):
    batch, seq, d = x.shape
    depth = w_ada.shape[0]
    assert batch == 1 and seq % ROW_TILE == 0
    ret_w = ret_gn.shape[1]
    swa_w = sinks.shape[1] * HEAD_DIM
    conv_c = conv_w.shape[2]
    kv_w = SWA_KV_HEADS * HEAD_DIM
    assert w_in.shape[2] == 4 * ret_w + swa_w + 2 * kv_w + 3 * conv_c
    assert ret_w % LANES == 0 and swa_w % LANES == 0 and kv_w == LANES and conv_c % LANES == 0
    assert w_down.shape[1] % FFN_COL == 0 and w_ada.shape[2] % ADA_COL == 0
    dims = (ret_w, swa_w, kv_w, conv_c)

    mod = _ada_modulation(c, w_ada, b_ada)[:, 0, :].reshape(depth, 6, 1, d)
    cos_t, sin_t = _rope_tables(seq)
    tables = _retention_tables(ret_w // HEAD_DIM)
    reps = LANES // HEAD_DIM
    w_in_b, w_out_b = w_in.astype(BF16), w_out.astype(BF16)
    w_gu_b, w_down_b = w_gu.astype(BF16), w_down.astype(BF16)

    convp = jnp.concatenate([conv_w, conv_b[:, None, :]], axis=1)
    qn = jnp.tile(q_norm, (1, reps))[:, None, :]
    kn = jnp.tile(k_norm, (1, reps))[:, None, :]

    xs = x[0]
    for l in range(depth):
        xs = _mixer(l, xs, sinks, mod, g_mix[:, None, :], w_in_b, cos_t, sin_t, convp, qn, kn,
                    ret_gn[:, None, :], tables, w_out_b, dims)
        xs = _ffn(l, xs, mod, g_ffn[:, None, :], w_gu_b, w_down_b)
    return xs[None]
```

```python
import functools

import jax
import jax.numpy as jnp
from jax import lax
from jax.experimental import pallas as pl
from jax.experimental.pallas import tpu as pltpu

HEAD_DIM = 64
LANES = 128
SWA_KV_HEADS = 2
CONV_K = 3
CHUNK = 128
ROPE_THETA = 10000.0
EPS = 1e-6
NEG_INF = -1e30
ROW_TILE = 512
FFN_COL = 256
FILL_BEFORE_PROJ = 3
ADA_COL = 1024
VMEM_LIMIT_BYTES = 60 * 1024 * 1024

F32 = jnp.float32
BF16 = jnp.bfloat16


def _dot(a, b):
    return jnp.dot(a, b, preferred_element_type=F32)


def _dot_nt(a, b):
    return lax.dot_general(a, b, (((1,), (1,)), ((), ())), preferred_element_type=F32)


def _dot_tn(a, b):
    return lax.dot_general(a, b, (((0,), (0,)), ((), ())), preferred_element_type=F32)


def _lane_lo(shape):
    return lax.broadcasted_iota(jnp.int32, shape, 1) % LANES < HEAD_DIM


def _head_sum(a, lo):
    s_lo = jnp.sum(jnp.where(lo, a, 0.0), axis=-1, keepdims=True)
    s_hi = jnp.sum(jnp.where(lo, 0.0, a), axis=-1, keepdims=True)
    return jnp.where(lo, s_lo, s_hi)


def _rope(a, cos, sin_signed, first_half):
    width = a.shape[-1]
    fwd = pltpu.roll(a, width - HEAD_DIM // 2, 1)
    bwd = pltpu.roll(a, HEAD_DIM // 2, 1)
    return a * cos + jnp.where(first_half, fwd, bwd) * sin_signed


def _silu(a):
    return a * jax.nn.sigmoid(a)


def _modulated_rms_norm(x, gain_scale, shift):
    ms = jnp.mean(x * x, axis=-1, keepdims=True)
    return (x * lax.rsqrt(ms + EPS) * gain_scale + shift).astype(BF16)


def _ada_kernel(c_ref, w_ref, b_ref, o_ref):
    s = _silu(c_ref[...])
    w = w_ref[0]
    s_hi = s.astype(BF16)
    s_lo = (s - s_hi.astype(F32)).astype(BF16)
    w_hi = w.astype(BF16)
    w_lo = (w - w_hi.astype(F32)).astype(BF16)
    acc = _dot(s_hi, w_lo) + _dot(s_lo, w_hi) + _dot(s_hi, w_hi)
    o_ref[0] = acc + b_ref[0]


def _ada_modulation(c, w_ada, b_ada):
    depth, d, e = w_ada.shape
    c8 = jnp.broadcast_to(c, (8, d))
    return pl.pallas_call(
        _ada_kernel,
        grid=(depth, e // ADA_COL),
        in_specs=[
            pl.BlockSpec((8, d), lambda l, j: (0, 0)),
            pl.BlockSpec((1, d, ADA_COL), lambda l, j: (l, 0, j)),
            pl.BlockSpec((1, 1, ADA_COL), lambda l, j: (l, 0, j)),
        ],
        out_specs=pl.BlockSpec((1, 8, ADA_COL), lambda l, j: (l, 0, j)),
        out_shape=jax.ShapeDtypeStruct((depth, 8, e), F32),
        compiler_params=pltpu.CompilerParams(
            dimension_semantics=("arbitrary", "arbitrary"), vmem_limit_bytes=VMEM_LIMIT_BYTES),
        name="ada_modulation",
    )(c8, w_ada, b_ada.reshape(depth, 1, e))


def _mixer_tile(dims, layer, first_tile, x, mod_ref, sinks_ref, gmix_ref, win_ref, cos_ref, sin_ref, convp_ref,
                qn_ref, kn_ref, gn_ref, dpair_ref, qdec_ref, kdec_ref, sdec_ref, wout_ref,
                proj_s, mix_s, state_s, kprev_s, vprev_s, zbuf_s, fill):
    ret_w, swa_w, kv_w, conv_c = dims
    n_pairs = ret_w // LANES
    n_qcols = swa_w // LANES
    group = (swa_w // HEAD_DIM) // SWA_KV_HEADS
    tile = x.shape[0]

    off_rq, off_rk, off_rv, off_rg = 0, ret_w, 2 * ret_w, 3 * ret_w
    off_aq = 4 * ret_w
    off_ak = off_aq + swa_w
    off_av = off_ak + kv_w
    off_cb = off_av + kv_w
    off_cc = off_cb + conv_c
    off_cu = off_cc + conv_c

    shift1, scale1, gate1 = mod_ref[0, 0], mod_ref[0, 1], mod_ref[0, 2]
    h = _modulated_rms_norm(x, gmix_ref[0] * (1.0 + scale1), shift1)
    fill(FILL_BEFORE_PROJ)
    proj_s[...] = _dot(h, win_ref[0])

    sq = (CHUNK, LANES)
    lo = _lane_lo(sq)
    first_half = lax.broadcasted_iota(jnp.int32, sq, 1) % HEAD_DIM < HEAD_DIM // 2
    row = lax.broadcasted_iota(jnp.int32, sq, 0)
    col = lax.broadcasted_iota(jnp.int32, sq, 1)
    block_diag = (row < HEAD_DIM) == (col < HEAD_DIM)
    qi = lax.broadcasted_iota(jnp.int32, (CHUNK, 2 * CHUNK), 0)
    kj = lax.broadcasted_iota(jnp.int32, (CHUNK, 2 * CHUNK), 1)
    band = (kj > qi) & (kj <= qi + CHUNK)
    zero_b = jnp.zeros(sq, BF16)
    ones_b = jnp.ones((2 * CHUNK, LANES), BF16)

    def blk(off, c, width=LANES):
        return (pl.ds(c * CHUNK, CHUNK), pl.ds(off, width))

    states = [state_s[p] for p in range(n_pairs)]
    k_prev = [kprev_s[grp] for grp in range(SWA_KV_HEADS)]
    v_prev = [vprev_s[grp] for grp in range(SWA_KV_HEADS)]

    for c in range(tile // CHUNK):
        rows = pl.ds(c * CHUNK, CHUNK)
        cos = cos_ref[rows, :]
        sin = sin_ref[rows, :]

        qs, ks, vs, scores = [], [], [], []
        for p in range(n_pairs):
            q = _rope(proj_s[blk(off_rq + p * LANES, c)], cos, sin, first_half)
            k = _rope(proj_s[blk(off_rk + p * LANES, c)], cos, sin, first_half) * (HEAD_DIM ** -0.5)
            v = proj_s[blk(off_rv + p * LANES, c)].astype(BF16)
            kb = k.astype(BF16)
            k_bd = jnp.concatenate([jnp.where(lo, kb, zero_b), jnp.where(lo, zero_b, kb)], axis=0)
            scores.append(_dot_nt(q.astype(BF16), k_bd) * dpair_ref[p])
            qs.append(q)
            ks.append(k)
            vs.append(v)
        fill(1)
        for p in range(n_pairs):
            q, k, v = qs[p], ks[p], vs[p]
            g = proj_s[blk(off_rg + p * LANES, c)]
            v_bd = jnp.concatenate([jnp.where(lo, v, zero_b), jnp.where(lo, zero_b, v)], axis=0)
            inner = _dot(scores[p].astype(BF16), v_bd)
            cross = _dot((q * qdec_ref[p]).astype(BF16), states[p].astype(BF16))
            kv = _dot_tn((k * kdec_ref[p]).astype(BF16), v)
            states[p] = states[p] * sdec_ref[p] + jnp.where(block_diag, kv, 0.0)
            o = inner + cross
            mu = _head_sum(o, lo) * (1.0 / HEAD_DIM)
            d = o - mu
            var = _head_sum(d * d, lo) * (1.0 / HEAD_DIM)
            o = d * lax.rsqrt(var + EPS) * gn_ref[0, :, pl.ds(p * LANES, LANES)]
            mix_s[blk(p * LANES, c)] = (o * _silu(g)).astype(BF16)

        def head_norm(a, gain):
            ms = _head_sum(a * a, lo) * (1.0 / HEAD_DIM)
            return a * lax.rsqrt(ms + EPS) * gain

        k = _rope(head_norm(proj_s[blk(off_ak, c)], kn_ref[0]), cos, sin, first_half)
        v = proj_s[blk(off_av, c)]
        k_sw = pltpu.roll(k, HEAD_DIM, 1)
        v_sw = pltpu.roll(v, HEAD_DIM, 1)
        k_cur = [jnp.where(lo, k, k_sw).astype(BF16), jnp.where(lo, k_sw, k).astype(BF16)]
        v_cur = [jnp.where(lo, v, v_sw).astype(BF16), jnp.where(lo, v_sw, v).astype(BF16)]
        qcols = [
            _rope(head_norm(proj_s[blk(off_aq + j * LANES, c)], qn_ref[0]), cos, sin, first_half)
            * (HEAD_DIM ** -0.5)
            for j in range(n_qcols)
        ]
        if c == 0:
            first_key = jnp.where(first_tile, CHUNK, 0)
            mask = band & (kj >= first_key)
        else:
            mask = band
        heads_out, swa_scores = [], []
        for grp in range(SWA_KV_HEADS):
            heads = [grp * group + j for j in range(group)]
            q_st = jnp.concatenate(
                [jnp.where(lo if hd % 2 == 0 else ~lo, qcols[hd // 2], 0.0).astype(BF16) for hd in heads],
                axis=0)
            k_win = jnp.concatenate([k_prev[grp], k_cur[grp]], axis=0)
            swa_scores.append(_dot_nt(q_st, k_win))
        fill(1)
        for grp in range(SWA_KV_HEADS):
            heads = [grp * group + j for j in range(group)]
            v_win = jnp.concatenate(
                [jnp.concatenate([v_prev[grp], v_cur[grp]], axis=0), ones_b], axis=1)
            sc = swa_scores[grp]
            probs, maxes = [], []
            for j, hd in enumerate(heads):
                s_h = jnp.where(mask, sc[j * CHUNK:(j + 1) * CHUNK], NEG_INF)
                m = jnp.maximum(jnp.max(s_h, axis=-1, keepdims=True), sinks_ref[layer, hd])
                probs.append(jnp.exp(s_h - m).astype(BF16))
                maxes.append(m)
            ov = _dot(jnp.concatenate(probs, axis=0), v_win)
            for j, hd in enumerate(heads):
                o_h = ov[j * CHUNK:(j + 1) * CHUNK]
                denom = o_h[:, LANES:] + jnp.exp(sinks_ref[layer, hd] - maxes[j])
                heads_out.append(o_h[:, :LANES] / denom)
        for j in range(n_qcols):
            mix_s[blk(ret_w + j * LANES, c)] = jnp.where(lo, heads_out[2 * j], heads_out[2 * j + 1]).astype(BF16)
        k_prev, v_prev = k_cur, v_cur

        z = proj_s[blk(off_cc, c, conv_c)] * proj_s[blk(off_cu, c, conv_c)]
        zbuf_s[pl.ds(8 + c * CHUNK, CHUNK), :] = z
        z1 = zbuf_s[pl.ds(7 + c * CHUNK, CHUNK), :]
        z2 = zbuf_s[pl.ds(6 + c * CHUNK, CHUNK), :]
        y = (z2 * convp_ref[0, 0:1, :] + z1 * convp_ref[0, 1:2, :] + z * convp_ref[0, 2:3, :]
             + convp_ref[0, 3:4, :])
        mix_s[blk(ret_w + swa_w, c, conv_c)] = (proj_s[blk(off_cb, c, conv_c)] * y).astype(BF16)

    for p in range(n_pairs):
        state_s[p] = states[p]
    for grp in range(SWA_KV_HEADS):
        kprev_s[grp] = k_prev[grp]
        vprev_s[grp] = v_prev[grp]
    zbuf_s[0:8, :] = zbuf_s[tile:tile + 8, :]
    return x + gate1 * _dot(mix_s[...], wout_ref[0])


def _ffn_slab(j, h, wgu_ref, act_s):
    hidden = act_s.shape[1]
    cols = pl.ds(j * FFN_COL, FFN_COL)
    gt = _dot(h, wgu_ref[0, :, cols])
    up = _dot(h, wgu_ref[0, :, pl.ds(hidden + j * FFN_COL, FFN_COL)])
    act_s[:, cols] = (_silu(gt) * up).astype(BF16)


def _layer_kernel(dims, layer, sinks_ref, x_ref, mod_ref, gmix_ref, win_ref, cos_ref, sin_ref, convp_ref,
                  qn_ref, kn_ref, gn_ref, dpair_ref, qdec_ref, kdec_ref, sdec_ref, wout_ref,
                  gffn_ref, wgu_ref, wdown_ref, o_ref,
                  proj_s, mix_s, state_s, kprev_s, vprev_s, zbuf_s, xmid_s, xnew_s, act_s):
    step = pl.program_id(0)

    @pl.when(step == 0)
    def _init():
        state_s[...] = jnp.zeros_like(state_s)
        kprev_s[...] = jnp.zeros_like(kprev_s)
        vprev_s[...] = jnp.zeros_like(vprev_s)
        zbuf_s[0:8, :] = jnp.zeros((8, zbuf_s.shape[1]), F32)
        xnew_s[...] = jnp.zeros_like(xnew_s)

    xmid_s[...] = xnew_s[...]
    shift2, scale2, gate2 = mod_ref[0, 3], mod_ref[0, 4], mod_ref[0, 5]
    h2 = _modulated_rms_norm(xmid_s[...], gffn_ref[0] * (1.0 + scale2), shift2)
    slabs = iter(range(act_s.shape[1] // FFN_COL))

    def fill(n):
        for _ in range(n):
            j = next(slabs, None)
            if j is not None:
                _ffn_slab(j, h2, wgu_ref, act_s)

    xnew_s[...] = _mixer_tile(
        dims, layer, step == 0, x_ref[...], mod_ref, sinks_ref, gmix_ref, win_ref, cos_ref, sin_ref, convp_ref,
        qn_ref, kn_ref, gn_ref, dpair_ref, qdec_ref, kdec_ref, sdec_ref, wout_ref,
        proj_s, mix_s, state_s, kprev_s, vprev_s, zbuf_s, fill)
    fill(act_s.shape[1] // FFN_COL)
    o_ref[...] = xmid_s[...] + gate2 * _dot(act_s[...], wdown_ref[0])


def _fixed_block(block, index):
    return pl.BlockSpec(block, lambda i: index, pipeline_mode=pl.Buffered(1))


def _whole(arr):
    return _fixed_block(arr.shape, (0,) * arr.ndim)


def _layer_slice(arr, layer):
    return _fixed_block((1,) + arr.shape[1:], (layer,) + (0,) * (arr.ndim - 1))


def _layer(layer, x, sinks, mod, g_mix, w_in, cos_t, sin_t, convp, qn, kn, gn, tables, w_out, g_ffn, w_gu,
           w_down, dims):
    seq, d = x.shape
    ret_w, swa_w, kv_w, conv_c = dims
    in_w = w_in.shape[2]
    hidden = w_down.shape[1]
    n_pairs = ret_w // LANES
    dpair, qdec, kdec, sdec = tables
    tile = ROW_TILE
    n_tiles = seq // tile

    def mixer_rows(i):
        return (jnp.minimum(i, n_tiles - 1), 0)

    def ffn_rows(i):
        return (jnp.maximum(i - 1, 0), 0)

    return pl.pallas_call(
        functools.partial(_layer_kernel, dims, layer),
        grid=(n_tiles + 1,),
        in_specs=[
            pl.BlockSpec(memory_space=pltpu.SMEM),
            pl.BlockSpec((tile, d), mixer_rows),
            _layer_slice(mod, layer),
            _layer_slice(g_mix, layer),
            _layer_slice(w_in, layer),
            pl.BlockSpec((tile, LANES), mixer_rows),
            pl.BlockSpec((tile, LANES), mixer_rows),
            _layer_slice(convp, layer),
            _layer_slice(qn, layer),
            _layer_slice(kn, layer),
            _layer_slice(gn, layer),
            _whole(dpair),
            _whole(qdec),
            _whole(kdec),
            _whole(sdec),
            _layer_slice(w_out, layer),
            _layer_slice(g_ffn, layer),
            _layer_slice(w_gu, layer),
            _layer_slice(w_down, layer),
        ],
        out_specs=pl.BlockSpec((tile, d), ffn_rows),
        out_shape=jax.ShapeDtypeStruct((seq, d), F32),
        scratch_shapes=[
            pltpu.VMEM((tile, in_w), F32),
            pltpu.VMEM((tile, d), BF16),
            pltpu.VMEM((n_pairs, LANES, LANES), F32),
            pltpu.VMEM((SWA_KV_HEADS, CHUNK, LANES), BF16),
            pltpu.VMEM((SWA_KV_HEADS, CHUNK, LANES), BF16),
            pltpu.VMEM((tile + 8, conv_c), F32),
            pltpu.VMEM((tile, d), F32),
            pltpu.VMEM((tile, d), F32),
            pltpu.VMEM((tile, hidden), BF16),
        ],
        compiler_params=pltpu.CompilerParams(
            dimension_semantics=("arbitrary",), vmem_limit_bytes=VMEM_LIMIT_BYTES),
        name="layer",
    )(sinks, x, mod, g_mix, w_in, cos_t, sin_t, convp, qn, kn, gn, dpair, qdec, kdec, sdec, w_out,
      g_ffn, w_gu, w_down)


def _rope_tables(seq):
    inv = 1.0 / (ROPE_THETA ** (jnp.arange(0, HEAD_DIM, 2, dtype=F32) / HEAD_DIM))
    half = HEAD_DIM // 2
    lane = jnp.arange(LANES)
    ang = jnp.arange(seq, dtype=F32)[:, None] * inv[lane % half][None, :]
    sign = jnp.where(lane % HEAD_DIM < half, -1.0, 1.0).astype(F32)
    return jnp.cos(ang), jnp.sin(ang) * sign[None, :]


def _retention_tables(ret_heads):
    log_gamma = jnp.log(1.0 - 2.0 ** (-5.0 - jnp.arange(ret_heads, dtype=F32)))
    idx = jnp.arange(CHUNK, dtype=F32)
    rel = idx[:, None] - idx[None, :]
    intra = jnp.where(rel >= 0, jnp.exp(log_gamma[:, None, None] * jnp.maximum(rel, 0.0)), 0.0)
    k_decay = jnp.exp(log_gamma[:, None] * (CHUNK - 1.0 - idx)[None, :])
    q_decay = jnp.exp(log_gamma[:, None] * (idx + 1.0)[None, :])
    chunk_decay = jnp.exp(log_gamma * CHUNK)
    n_pairs = ret_heads // 2

    def lanes(per_head):
        wide = jnp.repeat(per_head[:, :, None], HEAD_DIM, axis=2)
        return jnp.concatenate([wide[0::2], wide[1::2]], axis=2)

    dpair = jnp.concatenate([intra[0::2], intra[1::2]], axis=2)
    sdec = jnp.broadcast_to(
        jnp.repeat(chunk_decay.reshape(n_pairs, 2), HEAD_DIM, axis=1)[:, :, None], (n_pairs, LANES, LANES))
    return dpair, lanes(q_decay), lanes(k_decay), sdec


@jax.jit
def kernel(x, c, w_ada, b_ada, g_mix, w_in, conv_w, conv_b, q_norm, k_norm, sinks, ret_gn, w_out, g_ffn,
           w_gu, w_down):
    batch, seq, d = x.shape
    depth = w_ada.shape[0]
    assert batch == 1 and seq % ROW_TILE == 0
    ret_w = ret_gn.shape[1]
    swa_w = sinks.shape[1] * HEAD_DIM
    conv_c = conv_w.shape[2]
    kv_w = SWA_KV_HEADS * HEAD_DIM
    assert w_in.shape[2] == 4 * ret_w + swa_w + 2 * kv_w + 3 * conv_c
    assert ret_w % LANES == 0 and swa_w % LANES == 0 and kv_w == LANES and conv_c % LANES == 0
    assert w_down.shape[1] % FFN_COL == 0 and w_ada.shape[2] % ADA_COL == 0
    dims = (ret_w, swa_w, kv_w, conv_c)

    mod = _ada_modulation(c, w_ada, b_ada)[:, 0, :].reshape(depth, 6, 1, d)
    cos_t, sin_t = _rope_tables(seq)
    tables = _retention_tables(ret_w // HEAD_DIM)
    reps = LANES // HEAD_DIM
    w_in_b, w_out_b = w_in.astype(BF16), w_out.astype(BF16)
    w_gu_b, w_down_b = w_gu.astype(BF16), w_down.astype(BF16)
    convp = jnp.concatenate([conv_w, conv_b[:, None, :]], axis=1)
    qn = jnp.tile(q_norm, (1, reps))[:, None, :]
    kn = jnp.tile(k_norm, (1, reps))[:, None, :]

    xs = x[0]
    for l in range(depth):
        xs = _layer(l, xs, sinks, mod, g_mix[:, None, :], w_in_b, cos_t, sin_t, convp, qn, kn,
                    ret_gn[:, None, :], tables, w_out_b, g_ffn[:, None, :], w_gu_b, w_down_b, dims)
    return xs[None]
```

```python
import functools

import jax
import jax.numpy as jnp
from jax import lax
from jax.experimental import pallas as pl
from jax.experimental.pallas import tpu as pltpu

HEAD_DIM = 64
LANES = 128
SWA_KV_HEADS = 2
CONV_K = 3
CHUNK = 128
ROPE_THETA = 10000.0
EPS = 1e-6
NEG_INF = -1e30
ROW_TILE = 512
FFN_COL = 256
FILL_BEFORE_PROJ = 3
ADA_COL = 1024
VMEM_LIMIT_BYTES = 60 * 1024 * 1024

F32 = jnp.float32
BF16 = jnp.bfloat16


def _dot(a, b):
    return jnp.dot(a, b, preferred_element_type=F32)


def _dot_nt(a, b):
    return lax.dot_general(a, b, (((1,), (1,)), ((), ())), preferred_element_type=F32)


def _dot_tn(a, b):
    return lax.dot_general(a, b, (((0,), (0,)), ((), ())), preferred_element_type=F32)


def _lane_lo(shape):
    return lax.broadcasted_iota(jnp.int32, shape, 1) % LANES < HEAD_DIM


def _head_sum(a, lo):
    s_lo = jnp.sum(jnp.where(lo, a, 0.0), axis=-1, keepdims=True)
    s_hi = jnp.sum(jnp.where(lo, 0.0, a), axis=-1, keepdims=True)
    return jnp.where(lo, s_lo, s_hi)


def _rope(a, cos, sin_signed, first_half):
    width = a.shape[-1]
    fwd = pltpu.roll(a, width - HEAD_DIM // 2, 1)
    bwd = pltpu.roll(a, HEAD_DIM // 2, 1)
    return a * cos + jnp.where(first_half, fwd, bwd) * sin_signed


def _silu(a):
    return a * jax.nn.sigmoid(a)


def _modulated_rms_norm(x, gain_scale, shift):
    ms = jnp.mean(x * x, axis=-1, keepdims=True)
    return (x * lax.rsqrt(ms + EPS) * gain_scale + shift).astype(BF16)


def _ada_kernel(c_ref, w_ref, b_ref, o_ref):
    s = _silu(c_ref[...])
    w = w_ref[0]
    s_hi = s.astype(BF16)
    s_lo = (s - s_hi.astype(F32)).astype(BF16)
    w_hi = w.astype(BF16)
    w_lo = (w - w_hi.astype(F32)).astype(BF16)
    acc = _dot(s_hi, w_lo) + _dot(s_lo, w_hi) + _dot(s_hi, w_hi)
    o_ref[0] = acc + b_ref[0]


def _ada_modulation(c, w_ada, b_ada):
    depth, d, e = w_ada.shape
    c8 = jnp.broadcast_to(c, (8, d))
    return pl.pallas_call(
        _ada_kernel,
        grid=(depth, e // ADA_COL),
        in_specs=[
            pl.BlockSpec((8, d), lambda l, j: (0, 0)),
            pl.BlockSpec((1, d, ADA_COL), lambda l, j: (l, 0, j)),
            pl.BlockSpec((1, 1, ADA_COL), lambda l, j: (l, 0, j)),
        ],
        out_specs=pl.BlockSpec((1, 8, ADA_COL), lambda l, j: (l, 0, j)),
        out_shape=jax.ShapeDtypeStruct((depth, 8, e), F32),
        compiler_params=pltpu.CompilerParams(
            dimension_semantics=("arbitrary", "arbitrary"), vmem_limit_bytes=VMEM_LIMIT_BYTES),
        name="ada_modulation",
    )(c8, w_ada, b_ada.reshape(depth, 1, e))


def _mixer_tile(dims, layer, first_tile, x, mod_ref, sinks_ref, gmix_ref, win_ref, cos_ref, sin_ref, convp_ref,
                qn_ref, kn_ref, gn_ref, dpair_ref, qdec_ref, kdec_ref, sdec_ref, wout_ref,
                proj_s, mix_s, state_s, kprev_s, vprev_s, zbuf_s, fill):
    ret_w, swa_w, kv_w, conv_c = dims
    n_pairs = ret_w // LANES
    n_qcols = swa_w // LANES
    group = (swa_w // HEAD_DIM) // SWA_KV_HEADS
    tile = x.shape[0]

    off_rq, off_rk, off_rv, off_rg = 0, ret_w, 2 * ret_w, 3 * ret_w
    off_aq = 4 * ret_w
    off_ak = off_aq + swa_w
    off_av = off_ak + kv_w
    off_cb = off_av + kv_w
    off_cc = off_cb + conv_c
    off_cu = off_cc + conv_c

    shift1, scale1, gate1 = mod_ref[0, 0], mod_ref[0, 1], mod_ref[0, 2]
    h = _modulated_rms_norm(x, gmix_ref[0] * (1.0 + scale1), shift1)
    fill(FILL_BEFORE_PROJ)
    proj_s[...] = _dot(h, win_ref[0])

    sq = (CHUNK, LANES)
    lo = _lane_lo(sq)
    first_half = lax.broadcasted_iota(jnp.int32, sq, 1) % HEAD_DIM < HEAD_DIM // 2
    row = lax.broadcasted_iota(jnp.int32, sq, 0)
    col = lax.broadcasted_iota(jnp.int32, sq, 1)
    block_diag = (row < HEAD_DIM) == (col < HEAD_DIM)
    qi = lax.broadcasted_iota(jnp.int32, (CHUNK, 2 * CHUNK), 0)
    kj = lax.broadcasted_iota(jnp.int32, (CHUNK, 2 * CHUNK), 1)
    band = (kj > qi) & (kj <= qi + CHUNK)
    zero_b = jnp.zeros(sq, BF16)
    ones_b = jnp.ones((2 * CHUNK, LANES), BF16)

    def blk(off, c, width=LANES):
        return (pl.ds(c * CHUNK, CHUNK), pl.ds(off, width))

    states = [state_s[p] for p in range(n_pairs)]
    k_prev = [kprev_s[grp] for grp in range(SWA_KV_HEADS)]
    v_prev = [vprev_s[grp] for grp in range(SWA_KV_HEADS)]

    for c in range(tile // CHUNK):
        rows = pl.ds(c * CHUNK, CHUNK)
        cos = cos_ref[rows, :]
        sin = sin_ref[rows, :]

        qs, ks, vs, scores = [], [], [], []
        for p in range(n_pairs):
            q = _rope(proj_s[blk(off_rq + p * LANES, c)], cos, sin, first_half)
            k = _rope(proj_s[blk(off_rk + p * LANES, c)], cos, sin, first_half) * (HEAD_DIM ** -0.5)
            v = proj_s[blk(off_rv + p * LANES, c)].astype(BF16)
            kb = k.astype(BF16)
            k_bd = jnp.concatenate([jnp.where(lo, kb, zero_b), jnp.where(lo, zero_b, kb)], axis=0)
            scores.append(_dot_nt(q.astype(BF16), k_bd) * dpair_ref[p])
            qs.append(q)
            ks.append(k)
            vs.append(v)
        fill(1)
        for p in range(n_pairs):
            q, k, v = qs[p], ks[p], vs[p]
            g = proj_s[blk(off_rg + p * LANES, c)]
            v_bd = jnp.concatenate([jnp.where(lo, v, zero_b), jnp.where(lo, zero_b, v)], axis=0)
            inner = _dot(scores[p].astype(BF16), v_bd)
            cross = _dot((q * qdec_ref[p]).astype(BF16), states[p].astype(BF16))
            kv = _dot_tn((k * kdec_ref[p]).astype(BF16), v)
            states[p] = states[p] * sdec_ref[p] + jnp.where(block_diag, kv, 0.0)
            o = inner + cross
            mu = _head_sum(o, lo) * (1.0 / HEAD_DIM)
            d = o - mu
            var = _head_sum(d * d, lo) * (1.0 / HEAD_DIM)
            o = d * lax.rsqrt(var + EPS) * gn_ref[0, :, pl.ds(p * LANES, LANES)]
            mix_s[blk(p * LANES, c)] = (o * _silu(g)).astype(BF16)

        def head_norm(a, gain):
            ms = _head_sum(a * a, lo) * (1.0 / HEAD_DIM)
            return a * lax.rsqrt(ms + EPS) * gain

        k = _rope(head_norm(proj_s[blk(off_ak, c)], kn_ref[0]), cos, sin, first_half)
        v = proj_s[blk(off_av, c)]
        k_sw = pltpu.roll(k, HEAD_DIM, 1)
        v_sw = pltpu.roll(v, HEAD_DIM, 1)
        k_cur = [jnp.where(lo, k, k_sw).astype(BF16), jnp.where(lo, k_sw, k).astype(BF16)]
        v_cur = [jnp.where(lo, v, v_sw).astype(BF16), jnp.where(lo, v_sw, v).astype(BF16)]
        qcols = [
            _rope(head_norm(proj_s[blk(off_aq + j * LANES, c)], qn_ref[0]), cos, sin, first_half)
            * (HEAD_DIM ** -0.5)
            for j in range(n_qcols)
        ]
        mask = band & (kj >= CHUNK) if (first_tile and c == 0) else band
        heads_out, swa_scores = [], []
        for grp in range(SWA_KV_HEADS):
            heads = [grp * group + j for j in range(group)]
            q_st = jnp.concatenate(
                [jnp.where(lo if hd % 2 == 0 else ~lo, qcols[hd // 2], 0.0).astype(BF16) for hd in heads],
                axis=0)
            k_win = jnp.concatenate([k_prev[grp], k_cur[grp]], axis=0)
            swa_scores.append(_dot_nt(q_st, k_win))
        fill(1)
        for grp in range(SWA_KV_HEADS):
            heads = [grp * group + j for j in range(group)]
            v_win = jnp.concatenate(
                [jnp.concatenate([v_prev[grp], v_cur[grp]], axis=0), ones_b], axis=1)
            sc = swa_scores[grp]
            probs, maxes = [], []
            for j, hd in enumerate(heads):
                s_h = jnp.where(mask, sc[j * CHUNK:(j + 1) * CHUNK], NEG_INF)
                m = jnp.maximum(jnp.max(s_h, axis=-1, keepdims=True), sinks_ref[layer, hd])
                probs.append(jnp.exp(s_h - m).astype(BF16))
                maxes.append(m)
            ov = _dot(jnp.concatenate(probs, axis=0), v_win)
            for j, hd in enumerate(heads):
                o_h = ov[j * CHUNK:(j + 1) * CHUNK]
                denom = o_h[:, LANES:] + jnp.exp(sinks_ref[layer, hd] - maxes[j])
                heads_out.append(o_h[:, :LANES] / denom)
        for j in range(n_qcols):
            mix_s[blk(ret_w + j * LANES, c)] = jnp.where(lo, heads_out[2 * j], heads_out[2 * j + 1]).astype(BF16)
        k_prev, v_prev = k_cur, v_cur

        z = proj_s[blk(off_cc, c, conv_c)] * proj_s[blk(off_cu, c, conv_c)]
        zbuf_s[pl.ds(8 + c * CHUNK, CHUNK), :] = z
        z1 = zbuf_s[pl.ds(7 + c * CHUNK, CHUNK), :]
        z2 = zbuf_s[pl.ds(6 + c * CHUNK, CHUNK), :]
        y = (z2 * convp_ref[0, 0:1, :] + z1 * convp_ref[0, 1:2, :] + z * convp_ref[0, 2:3, :]
             + convp_ref[0, 3:4, :])
        mix_s[blk(ret_w + swa_w, c, conv_c)] = (proj_s[blk(off_cb, c, conv_c)] * y).astype(BF16)

    for p in range(n_pairs):
        state_s[p] = states[p]
    for grp in range(SWA_KV_HEADS):
        kprev_s[grp] = k_prev[grp]
        vprev_s[grp] = v_prev[grp]
    zbuf_s[0:8, :] = zbuf_s[tile:tile + 8, :]
    return x + gate1 * _dot(mix_s[...], wout_ref[0])


def _ffn_slab(j, h, wgu_ref, act_s):
    hidden = act_s.shape[1]
    cols = pl.ds(j * FFN_COL, FFN_COL)
    gt = _dot(h, wgu_ref[0, :, cols])
    up = _dot(h, wgu_ref[0, :, pl.ds(hidden + j * FFN_COL, FFN_COL)])
    act_s[:, cols] = (_silu(gt) * up).astype(BF16)


def _layer_kernel(dims, layer, sinks_ref, x_ref, mod_ref, gmix_ref, win_ref, cos_ref, sin_ref, convp_ref,
                  qn_ref, kn_ref, gn_ref, dpair_ref, qdec_ref, kdec_ref, sdec_ref, wout_ref,
                  gffn_ref, wgu_ref, wdown_ref, o_ref,
                  proj_s, mix_s, state_s, kprev_s, vprev_s, zbuf_s, xmid_s, xnew_s, act_s):
    step = pl.program_id(0)
    last = pl.num_programs(0) - 1
    n_slabs = act_s.shape[1] // FFN_COL

    def body(do_ffn, do_mixer, first_tile):
        slabs = iter(range(n_slabs if do_ffn else 0))
        if do_ffn:
            xmid_s[...] = xnew_s[...]
            shift2, scale2, gate2 = mod_ref[0, 3], mod_ref[0, 4], mod_ref[0, 5]
            h2 = _modulated_rms_norm(xmid_s[...], gffn_ref[0] * (1.0 + scale2), shift2)

        def fill(n):
            for _ in range(n):
                j = next(slabs, None)
                if j is not None:
                    _ffn_slab(j, h2, wgu_ref, act_s)

        if do_mixer:
            xnew_s[...] = _mixer_tile(
                dims, layer, first_tile, x_ref[...], mod_ref, sinks_ref, gmix_ref, win_ref, cos_ref, sin_ref,
                convp_ref, qn_ref, kn_ref, gn_ref, dpair_ref, qdec_ref, kdec_ref, sdec_ref, wout_ref,
                proj_s, mix_s, state_s, kprev_s, vprev_s, zbuf_s, fill)
        if do_ffn:
            fill(n_slabs)
            o_ref[...] = xmid_s[...] + gate2 * _dot(act_s[...], wdown_ref[0])

    @pl.when(step == 0)
    def _first():
        state_s[...] = jnp.zeros_like(state_s)
        kprev_s[...] = jnp.zeros_like(kprev_s)
        vprev_s[...] = jnp.zeros_like(vprev_s)
        zbuf_s[0:8, :] = jnp.zeros((8, zbuf_s.shape[1]), F32)
        body(False, True, True)

    @pl.when(jnp.logical_and(step > 0, step < last))
    def _steady():
        body(True, True, False)

    @pl.when(step == last)
    def _last():
        body(True, False, False)


def _fixed_block(block, index):
    return pl.BlockSpec(block, lambda i: index, pipeline_mode=pl.Buffered(1))


def _whole(arr):
    return _fixed_block(arr.shape, (0,) * arr.ndim)


def _layer_slice(arr, layer):
    return _fixed_block((1,) + arr.shape[1:], (layer,) + (0,) * (arr.ndim - 1))


def _layer(layer, x, sinks, mod, g_mix, w_in, cos_t, sin_t, convp, qn, kn, gn, tables, w_out, g_ffn, w_gu,
           w_down, dims):
    seq, d = x.shape
    ret_w, swa_w, kv_w, conv_c = dims
    in_w = w_in.shape[2]
    hidden = w_down.shape[1]
    n_pairs = ret_w // LANES
    dpair, qdec, kdec, sdec = tables
    tile = ROW_TILE
    n_tiles = seq // tile

    def mixer_rows(i):
        return (jnp.minimum(i, n_tiles - 1), 0)

    def ffn_rows(i):
        return (jnp.maximum(i - 1, 0), 0)

    return pl.pallas_call(
        functools.partial(_layer_kernel, dims, layer),
        grid=(n_tiles + 1,),
        in_specs=[
            pl.BlockSpec(memory_space=pltpu.SMEM),
            pl.BlockSpec((tile, d), mixer_rows),
            _layer_slice(mod, layer),
            _layer_slice(g_mix, layer),
            _layer_slice(w_in, layer),
            pl.BlockSpec((tile, LANES), mixer_rows),
            pl.BlockSpec((tile, LANES), mixer_rows),
            _layer_slice(convp, layer),
            _layer_slice(qn, layer),
            _layer_slice(kn, layer),
            _layer_slice(gn, layer),
            _whole(dpair),
            _whole(qdec),
            _whole(kdec),
            _whole(sdec),
            _layer_slice(w_out, layer),
            _layer_slice(g_ffn, layer),
            _layer_slice(w_gu, layer),
            _layer_slice(w_down, layer),
        ],
        out_specs=pl.BlockSpec((tile, d), ffn_rows),
        out_shape=jax.ShapeDtypeStruct((seq, d), F32),
        scratch_shapes=[
            pltpu.VMEM((tile, in_w), F32),
            pltpu.VMEM((tile, d), BF16),
            pltpu.VMEM((n_pairs, LANES, LANES), F32),
            pltpu.VMEM((SWA_KV_HEADS, CHUNK, LANES), BF16),
            pltpu.VMEM((SWA_KV_HEADS, CHUNK, LANES), BF16),
            pltpu.VMEM((tile + 8, conv_c), F32),
            pltpu.VMEM((tile, d), F32),
            pltpu.VMEM((tile, d), F32),
            pltpu.VMEM((tile, hidden), BF16),
        ],
        compiler_params=pltpu.CompilerParams(
            dimension_semantics=("arbitrary",), vmem_limit_bytes=VMEM_LIMIT_BYTES),
        name="layer",
    )(sinks, x, mod, g_mix, w_in, cos_t, sin_t, convp, qn, kn, gn, dpair, qdec, kdec, sdec, w_out,
      g_ffn, w_gu, w_down)


def _rope_tables(seq):
    inv = 1.0 / (ROPE_THETA ** (jnp.arange(0, HEAD_DIM, 2, dtype=F32) / HEAD_DIM))
    half = HEAD_DIM // 2
    lane = jnp.arange(LANES)
    ang = jnp.arange(seq, dtype=F32)[:, None] * inv[lane % half][None, :]
    sign = jnp.where(lane % HEAD_DIM < half, -1.0, 1.0).astype(F32)
    return jnp.cos(ang), jnp.sin(ang) * sign[None, :]


def _retention_tables(ret_heads):
    log_gamma = jnp.log(1.0 - 2.0 ** (-5.0 - jnp.arange(ret_heads, dtype=F32)))
    idx = jnp.arange(CHUNK, dtype=F32)
    rel = idx[:, None] - idx[None, :]
    intra = jnp.where(rel >= 0, jnp.exp(log_gamma[:, None, None] * jnp.maximum(rel, 0.0)), 0.0)
    k_decay = jnp.exp(log_gamma[:, None] * (CHUNK - 1.0 - idx)[None, :])
    q_decay = jnp.exp(log_gamma[:, None] * (idx + 1.0)[None, :])
    chunk_decay = jnp.exp(log_gamma * CHUNK)
    n_pairs = ret_heads // 2

    def lanes(per_head):
        wide = jnp.repeat(per_head[:, :, None], HEAD_DIM, axis=2)
        return jnp.concatenate([wide[0::2], wide[1::2]], axis=2)

    dpair = jnp.concatenate([intra[0::2], intra[1::2]], axis=2)
    sdec = jnp.broadcast_to(
        jnp.repeat(chunk_decay.reshape(n_pairs, 2), HEAD_DIM, axis=1)[:, :, None], (n_pairs, LANES, LANES))
    return dpair, lanes(q_decay), lanes(k_decay), sdec


@jax.jit
def kernel(x, c, w_ada, b_ada, g_mix, w_in, conv_w, conv_b, q_norm, k_norm, sinks, ret_gn, w_out, g_ffn,
           w_gu, w_down):
    batch, seq, d = x.shape
    depth = w_ada.shape[0]
    assert batch == 1 and seq % ROW_TILE == 0
    ret_w = ret_gn.shape[1]
    swa_w = sinks.shape[1] * HEAD_DIM
    conv_c = conv_w.shape[2]
    kv_w = SWA_KV_HEADS * HEAD_DIM
    assert w_in.shape[2] == 4 * ret_w + swa_w + 2 * kv_w + 3 * conv_c
    assert ret_w % LANES == 0 and swa_w % LANES == 0 and kv_w == LANES and conv_c % LANES == 0
    assert w_down.shape[1] % FFN_COL == 0 and w_ada.shape[2] % ADA_COL == 0
    dims = (ret_w, swa_w, kv_w, conv_c)

    mod = _ada_modulation(c, w_ada, b_ada)[:, 0, :].reshape(depth, 6, 1, d)
    cos_t, sin_t = _rope_tables(seq)
    tables = _retention_tables(ret_w // HEAD_DIM)
    reps = LANES // HEAD_DIM
    w_in_b, w_out_b = w_in.astype(BF16), w_out.astype(BF16)
    w_gu_b, w_down_b = w_gu.astype(BF16), w_down.astype(BF16)
    convp = jnp.concatenate([conv_w, conv_b[:, None, :]], axis=1)
    qn = jnp.tile(q_norm, (1, reps))[:, None, :]
    kn = jnp.tile(k_norm, (1, reps))[:, None, :]

    xs = x[0]
    for l in range(depth):
        xs = _layer(l, xs, sinks, mod, g_mix[:, None, :], w_in_b, cos_t, sin_t, convp, qn, kn,
                    ret_gn[:, None, :], tables, w_out_b, g_ffn[:, None, :], w_gu_b, w_down_b, dims)
    return xs[None]
```

```python
import functools

import jax
import jax.numpy as jnp
from jax import lax
from jax.experimental import pallas as pl
from jax.experimental.pallas import tpu as pltpu

HEAD_DIM = 64
LANES = 128
BF16_SUBLANES = 16
SWA_KV_HEADS = 2
CONV_K = 3
CHUNK = 128
ROPE_THETA = 10000.0
EPS = 1e-6
NEG_INF = -1e30
ROW_TILE = 512
FFN_COL = 256
FILL_BEFORE_PROJ = 3
ADA_COL = 2048
VMEM_LIMIT_BYTES = 60 * 1024 * 1024

F32 = jnp.float32
BF16 = jnp.bfloat16


def _dot(a, b):
    return jnp.dot(a, b, preferred_element_type=F32)


def _dot_nt(a, b):
    return lax.dot_general(a, b, (((1,), (1,)), ((), ())), preferred_element_type=F32)


def _dot_tn(a, b):
    return lax.dot_general(a, b, (((0,), (0,)), ((), ())), preferred_element_type=F32)


def _lane_lo(shape):
    return lax.broadcasted_iota(jnp.int32, shape, 1) % LANES < HEAD_DIM


def _head_sum(a, lo):
    s_lo = jnp.sum(jnp.where(lo, a, 0.0), axis=-1, keepdims=True)
    s_hi = jnp.sum(jnp.where(lo, 0.0, a), axis=-1, keepdims=True)
    return jnp.where(lo, s_lo, s_hi)


def _rope(a, cos, sin_signed, first_half):
    width = a.shape[-1]
    fwd = pltpu.roll(a, width - HEAD_DIM // 2, 1)
    bwd = pltpu.roll(a, HEAD_DIM // 2, 1)
    return a * cos + jnp.where(first_half, fwd, bwd) * sin_signed


def _silu(a):
    return a * jax.nn.sigmoid(a)


def _modulated_rms_norm(x, gain_scale, shift):
    ms = jnp.mean(x * x, axis=-1, keepdims=True)
    return (x * lax.rsqrt(ms + EPS) * gain_scale + shift).astype(BF16)


def _ada_kernel(c_ref, w_ref, b_ref, o_ref):
    s = _silu(c_ref[...])
    w = w_ref[0]
    s_hi = s.astype(BF16)
    s_lo = (s - s_hi.astype(F32)).astype(BF16)
    w_hi = w.astype(BF16)
    w_lo = (w - w_hi.astype(F32)).astype(BF16)
    acc = _dot(s_hi, w_lo) + _dot(s_lo, w_hi) + _dot(s_hi, w_hi)
    o_ref[0] = acc + b_ref[0]


def _ada_modulation(c, w_ada, b_ada):
    depth, d, e = w_ada.shape
    c8 = jnp.broadcast_to(c, (8, d))
    return pl.pallas_call(
        _ada_kernel,
        grid=(depth, e // ADA_COL),
        in_specs=[
            pl.BlockSpec((8, d), lambda l, j: (0, 0)),
            pl.BlockSpec((1, d, ADA_COL), lambda l, j: (l, 0, j)),
            pl.BlockSpec((1, 1, ADA_COL), lambda l, j: (l, 0, j)),
        ],
        out_specs=pl.BlockSpec((1, 8, ADA_COL), lambda l, j: (l, 0, j)),
        out_shape=jax.ShapeDtypeStruct((depth, 8, e), F32),
        compiler_params=pltpu.CompilerParams(
            dimension_semantics=("arbitrary", "arbitrary"), vmem_limit_bytes=VMEM_LIMIT_BYTES),
        name="ada_modulation",
    )(c8, w_ada, b_ada.reshape(depth, 1, e))


def _mixer_tile(dims, layer, first_tile, x, mod_ref, sinks_ref, gmix_ref, win_ref, cs_ref, convp_ref,
                qn_ref, kn_ref, gn_ref, dpair_ref, qdec_ref, kdec_ref, sdec_ref, wout_ref,
                proj_s, mix_s, state_s, kprev_s, vprev_s, zbuf_s, fill):
    ret_w, swa_w, kv_w, conv_c = dims
    n_pairs = ret_w // LANES
    n_qcols = swa_w // LANES
    group = (swa_w // HEAD_DIM) // SWA_KV_HEADS
    tile = x.shape[0]

    off_rq, off_rk, off_rv, off_rg = 0, ret_w, 2 * ret_w, 3 * ret_w
    off_aq = 4 * ret_w
    off_ak = off_aq + swa_w
    off_av = off_ak + kv_w
    off_cb = off_av + kv_w
    off_cc = off_cb + conv_c
    off_cu = off_cc + conv_c

    shift1, scale1, gate1 = mod_ref[0, 0], mod_ref[0, 1], mod_ref[0, 2]
    h = _modulated_rms_norm(x, gmix_ref[0] * (1.0 + scale1), shift1)
    fill(FILL_BEFORE_PROJ)
    proj_s[...] = _dot(h, win_ref[0])

    sq = (CHUNK, LANES)
    lo = _lane_lo(sq)
    first_half = lax.broadcasted_iota(jnp.int32, sq, 1) % HEAD_DIM < HEAD_DIM // 2
    row = lax.broadcasted_iota(jnp.int32, sq, 0)
    col = lax.broadcasted_iota(jnp.int32, sq, 1)
    block_diag = (row < HEAD_DIM) == (col < HEAD_DIM)
    qi = lax.broadcasted_iota(jnp.int32, (CHUNK, 2 * CHUNK), 0)
    kj = lax.broadcasted_iota(jnp.int32, (CHUNK, 2 * CHUNK), 1)
    band = (kj > qi) & (kj <= qi + CHUNK)
    zero_b = jnp.zeros(sq, BF16)
    reps = LANES // (HEAD_DIM // 2)
    sin_sign = jnp.where(first_half, -1.0, 1.0)
    ones_b = jnp.ones((2 * CHUNK, LANES), BF16)

    def blk(off, c, width=LANES):
        return (pl.ds(c * CHUNK, CHUNK), pl.ds(off, width))

    states = [state_s[p] for p in range(n_pairs)]
    k_prev = [kprev_s[grp] for grp in range(SWA_KV_HEADS)]
    v_prev = [vprev_s[grp] for grp in range(SWA_KV_HEADS)]

    for c in range(tile // CHUNK):
        rows = pl.ds(c * CHUNK, CHUNK)
        cs = cs_ref[rows, :]
        cos = jnp.concatenate([cs[:, :HEAD_DIM // 2]] * reps, axis=1)
        sin = jnp.concatenate([cs[:, HEAD_DIM // 2:]] * reps, axis=1) * sin_sign

        qs, ks, vs, scores = [], [], [], []
        for p in range(n_pairs):
            q = _rope(proj_s[blk(off_rq + p * LANES, c)], cos, sin, first_half)
            k = _rope(proj_s[blk(off_rk + p * LANES, c)], cos, sin, first_half) * (HEAD_DIM ** -0.5)
            v = proj_s[blk(off_rv + p * LANES, c)].astype(BF16)
            kb = k.astype(BF16)
            k_bd = jnp.concatenate([jnp.where(lo, kb, zero_b), jnp.where(lo, zero_b, kb)], axis=0)
            scores.append(_dot_nt(q.astype(BF16), k_bd) * dpair_ref[p])
            qs.append(q)
            ks.append(k)
            vs.append(v)
        fill(1)
        for p in range(n_pairs):
            q, k, v = qs[p], ks[p], vs[p]
            g = proj_s[blk(off_rg + p * LANES, c)]
            v_bd = jnp.concatenate([jnp.where(lo, v, zero_b), jnp.where(lo, zero_b, v)], axis=0)
            inner = _dot(scores[p].astype(BF16), v_bd)
            cross = _dot((q * qdec_ref[p]).astype(BF16), states[p].astype(BF16))
            kv = _dot_tn((k * kdec_ref[p]).astype(BF16), v)
            states[p] = states[p] * sdec_ref[p] + jnp.where(block_diag, kv, 0.0)
            o = inner + cross
            mu = _head_sum(o, lo) * (1.0 / HEAD_DIM)
            d = o - mu
            var = _head_sum(d * d, lo) * (1.0 / HEAD_DIM)
            o = d * lax.rsqrt(var + EPS) * gn_ref[0, :, pl.ds(p * LANES, LANES)]
            mix_s[blk(p * LANES, c)] = (o * _silu(g)).astype(BF16)

        def head_norm(a, gain):
            ms = _head_sum(a * a, lo) * (1.0 / HEAD_DIM)
            return a * lax.rsqrt(ms + EPS) * gain

        k = _rope(head_norm(proj_s[blk(off_ak, c)], kn_ref[0]), cos, sin, first_half)
        v = proj_s[blk(off_av, c)]
        k_sw = pltpu.roll(k, HEAD_DIM, 1)
        v_sw = pltpu.roll(v, HEAD_DIM, 1)
        k_cur = [jnp.where(lo, k, k_sw).astype(BF16), jnp.where(lo, k_sw, k).astype(BF16)]
        v_cur = [jnp.where(lo, v, v_sw).astype(BF16), jnp.where(lo, v_sw, v).astype(BF16)]
        qcols = [
            _rope(head_norm(proj_s[blk(off_aq + j * LANES, c)], qn_ref[0]), cos, sin, first_half)
            * (HEAD_DIM ** -0.5)
            for j in range(n_qcols)
        ]
        if c == 0:
            first_key = jnp.where(first_tile, CHUNK, 0)
            mask = band & (kj >= first_key)
        else:
            mask = band
        heads_out, swa_scores = [], []
        for grp in range(SWA_KV_HEADS):
            heads = [grp * group + j for j in range(group)]
            q_st = jnp.concatenate(
                [jnp.where(lo if hd % 2 == 0 else ~lo, qcols[hd // 2], 0.0).astype(BF16) for hd in heads],
                axis=0)
            k_win = jnp.concatenate([k_prev[grp], k_cur[grp]], axis=0)
            swa_scores.append(_dot_nt(q_st, k_win))
        fill(1)
        for grp in range(SWA_KV_HEADS):
            heads = [grp * group + j for j in range(group)]
            v_win = jnp.concatenate(
                [jnp.concatenate([v_prev[grp], v_cur[grp]], axis=0), ones_b], axis=1)
            sc = swa_scores[grp]
            probs, maxes = [], []
            for j, hd in enumerate(heads):
                s_h = jnp.where(mask, sc[j * CHUNK:(j + 1) * CHUNK], NEG_INF)
                m = jnp.maximum(jnp.max(s_h, axis=-1, keepdims=True), sinks_ref[layer, hd])
                probs.append(jnp.exp(s_h - m).astype(BF16))
                maxes.append(m)
            ov = _dot(jnp.concatenate(probs, axis=0), v_win)
            for j, hd in enumerate(heads):
                o_h = ov[j * CHUNK:(j + 1) * CHUNK]
                denom = o_h[:, LANES:] + jnp.exp(sinks_ref[layer, hd] - maxes[j])
                heads_out.append(o_h[:, :LANES] / denom)
        for j in range(n_qcols):
            mix_s[blk(ret_w + j * LANES, c)] = jnp.where(lo, heads_out[2 * j], heads_out[2 * j + 1]).astype(BF16)
        k_prev, v_prev = k_cur, v_cur

        z = proj_s[blk(off_cc, c, conv_c)] * proj_s[blk(off_cu, c, conv_c)]
        zbuf_s[pl.ds(8 + c * CHUNK, CHUNK), :] = z
        z1 = zbuf_s[pl.ds(7 + c * CHUNK, CHUNK), :]
        z2 = zbuf_s[pl.ds(6 + c * CHUNK, CHUNK), :]
        y = (z2 * convp_ref[0, 0:1, :] + z1 * convp_ref[0, 1:2, :] + z * convp_ref[0, 2:3, :]
             + convp_ref[0, 3:4, :])
        mix_s[blk(ret_w + swa_w, c, conv_c)] = (proj_s[blk(off_cb, c, conv_c)] * y).astype(BF16)

    for p in range(n_pairs):
        state_s[p] = states[p]
    for grp in range(SWA_KV_HEADS):
        kprev_s[grp] = k_prev[grp]
        vprev_s[grp] = v_prev[grp]
    zbuf_s[0:8, :] = zbuf_s[tile:tile + 8, :]
    return x + gate1 * _dot(mix_s[...], wout_ref[0])


def _ffn_slab(j, h, wgu_ref, act_s):
    hidden = act_s.shape[1]
    cols = pl.ds(j * FFN_COL, FFN_COL)
    gt = _dot(h, wgu_ref[0, :, cols])
    up = _dot(h, wgu_ref[0, :, pl.ds(hidden + j * FFN_COL, FFN_COL)])
    act_s[:, cols] = (_silu(gt) * up).astype(BF16)


N_LAYER_INPUTS = 18


def _layer_kernel(dims, layer, n_cast, *refs):
    (sinks_ref, x_ref, mod_ref, gmix_ref, win_ref, cs_ref, convp_ref, qn_ref, kn_ref, gn_ref,
     dpair_ref, qdec_ref, kdec_ref, sdec_ref, wout_ref, gffn_ref, wgu_ref, wdown_ref) = refs[:N_LAYER_INPUTS]
    cast_in = refs[N_LAYER_INPUTS:N_LAYER_INPUTS + n_cast]
    o_ref = refs[N_LAYER_INPUTS + n_cast]
    cast_out = refs[N_LAYER_INPUTS + n_cast + 1:N_LAYER_INPUTS + 2 * n_cast + 1]
    (proj_s, mix_s, state_s, kprev_s, vprev_s, zbuf_s, xmid_s, xnew_s, act_s) = refs[N_LAYER_INPUTS + 2 * n_cast + 1:]
    cast_refs = tuple(zip(cast_in, cast_out))
    step = pl.program_id(0)

    @pl.when(step == 0)
    def _init():
        state_s[...] = jnp.zeros_like(state_s)
        kprev_s[...] = jnp.zeros_like(kprev_s)
        vprev_s[...] = jnp.zeros_like(vprev_s)
        zbuf_s[0:8, :] = jnp.zeros((8, zbuf_s.shape[1]), F32)
        xnew_s[...] = jnp.zeros_like(xnew_s)

    for src, dst in cast_refs:
        dst[...] = src[...].astype(BF16)

    xmid_s[...] = xnew_s[...]
    shift2, scale2, gate2 = mod_ref[0, 3], mod_ref[0, 4], mod_ref[0, 5]
    h2 = _modulated_rms_norm(xmid_s[...], gffn_ref[0] * (1.0 + scale2), shift2)
    slabs = iter(range(act_s.shape[1] // FFN_COL))

    def fill(n):
        for _ in range(n):
            j = next(slabs, None)
            if j is not None:
                _ffn_slab(j, h2, wgu_ref, act_s)

    xnew_s[...] = _mixer_tile(
        dims, layer, step == 0, x_ref[...], mod_ref, sinks_ref, gmix_ref, win_ref, cs_ref, convp_ref,
        qn_ref, kn_ref, gn_ref, dpair_ref, qdec_ref, kdec_ref, sdec_ref, wout_ref,
        proj_s, mix_s, state_s, kprev_s, vprev_s, zbuf_s, fill)
    fill(act_s.shape[1] // FFN_COL)
    o_ref[...] = xmid_s[...] + gate2 * _dot(act_s[...], wdown_ref[0])


def _fixed_block(block, index):
    return pl.BlockSpec(block, lambda i: index, pipeline_mode=pl.Buffered(1))


def _whole(arr):
    return _fixed_block(arr.shape, (0,) * arr.ndim)


def _layer_slice(arr, layer):
    return _fixed_block((1,) + arr.shape[1:], (layer,) + (0,) * (arr.ndim - 1))


def _cast_chunk_rows(rows, n_steps):
    return next(c for c in range(BF16_SUBLANES, rows + 1, BF16_SUBLANES)
                if rows % c == 0 and rows // c <= n_steps)


def _layer(layer, x, sinks, mod, g_mix, weights, cs_t, convp, qn, kn, gn, tables, g_ffn, next_f32, dims):
    seq, d = x.shape
    ret_w, swa_w, kv_w, conv_c = dims
    w_in, w_out, w_gu, w_down = weights
    in_w = w_in.shape[2]
    hidden = w_down.shape[1]
    n_pairs = ret_w // LANES
    dpair, qdec, kdec, sdec = tables
    tile = ROW_TILE
    n_tiles = seq // tile

    def mixer_rows(i):
        return (jnp.minimum(i, n_tiles - 1), 0)

    def ffn_rows(i):
        return (jnp.maximum(i - 1, 0), 0)

    def chunk_rows(i, which, last):
        return (which, jnp.minimum(i, last), 0)

    cast_in_specs, cast_out_specs, cast_out_shapes = [], [], []
    for w in next_f32:
        _, rows, cols = w.shape
        chunk = _cast_chunk_rows(rows, n_tiles)
        last = rows // chunk - 1
        cast_in_specs.append(
            pl.BlockSpec((1, chunk, cols), functools.partial(chunk_rows, which=layer + 1, last=last)))
        cast_out_specs.append(
            pl.BlockSpec((1, chunk, cols), functools.partial(chunk_rows, which=0, last=last)))
        cast_out_shapes.append(jax.ShapeDtypeStruct((1, rows, cols), BF16))

    outs = pl.pallas_call(
        functools.partial(_layer_kernel, dims, layer, len(next_f32)),
        grid=(n_tiles + 1,),
        in_specs=[
            pl.BlockSpec(memory_space=pltpu.SMEM),
            pl.BlockSpec((tile, d), mixer_rows),
            _layer_slice(mod, layer),
            _layer_slice(g_mix, layer),
            _whole(w_in),
            pl.BlockSpec((tile, HEAD_DIM), mixer_rows),
            _layer_slice(convp, layer),
            _layer_slice(qn, layer),
            _layer_slice(kn, layer),
            _layer_slice(gn, layer),
            _whole(dpair),
            _whole(qdec),
            _whole(kdec),
            _whole(sdec),
            _whole(w_out),
            _layer_slice(g_ffn, layer),
            _whole(w_gu),
            _whole(w_down),
        ] + cast_in_specs,
        out_specs=[pl.BlockSpec((tile, d), ffn_rows)] + cast_out_specs,
        out_shape=[jax.ShapeDtypeStruct((seq, d), F32)] + cast_out_shapes,
        scratch_shapes=[
            pltpu.VMEM((tile, in_w), F32),
            pltpu.VMEM((tile, d), BF16),
            pltpu.VMEM((n_pairs, LANES, LANES), F32),
            pltpu.VMEM((SWA_KV_HEADS, CHUNK, LANES), BF16),
            pltpu.VMEM((SWA_KV_HEADS, CHUNK, LANES), BF16),
            pltpu.VMEM((tile + 8, conv_c), F32),
            pltpu.VMEM((tile, d), F32),
            pltpu.VMEM((tile, d), F32),
            pltpu.VMEM((tile, hidden), BF16),
        ],
        compiler_params=pltpu.CompilerParams(
            dimension_semantics=("arbitrary",), vmem_limit_bytes=VMEM_LIMIT_BYTES),
        name="layer",
    )(sinks, x, mod, g_mix, w_in, cs_t, convp, qn, kn, gn, dpair, qdec, kdec, sdec, w_out,
      g_ffn, w_gu, w_down, *next_f32)
    return outs[0], tuple(outs[1:])


def _rope_table(seq):
    half = HEAD_DIM // 2
    per_row = LANES // half
    inv = 1.0 / (ROPE_THETA ** (jnp.arange(0, HEAD_DIM, 2, dtype=F32) / HEAD_DIM))
    lane = jnp.arange(LANES)
    pos = (jnp.arange(seq // per_row)[:, None] * per_row + (lane // half)[None, :]).astype(F32)
    ang = pos * inv[lane % half][None, :]
    return jnp.concatenate([jnp.cos(ang).reshape(seq, half), jnp.sin(ang).reshape(seq, half)], axis=1)


def _retention_tables(ret_heads):
    log_gamma = jnp.log(1.0 - 2.0 ** (-5.0 - jnp.arange(ret_heads, dtype=F32)))
    idx = jnp.arange(CHUNK, dtype=F32)
    rel = idx[:, None] - idx[None, :]
    intra = jnp.where(rel >= 0, jnp.exp(log_gamma[:, None, None] * jnp.maximum(rel, 0.0)), 0.0)
    k_decay = jnp.exp(log_gamma[:, None] * (CHUNK - 1.0 - idx)[None, :])
    q_decay = jnp.exp(log_gamma[:, None] * (idx + 1.0)[None, :])
    chunk_decay = jnp.exp(log_gamma * CHUNK)
    n_pairs = ret_heads // 2

    def lanes(per_head):
        wide = jnp.repeat(per_head[:, :, None], HEAD_DIM, axis=2)
        return jnp.concatenate([wide[0::2], wide[1::2]], axis=2)

    dpair = jnp.concatenate([intra[0::2], intra[1::2]], axis=2)
    sdec = jnp.broadcast_to(
        jnp.repeat(chunk_decay.reshape(n_pairs, 2), HEAD_DIM, axis=1)[:, :, None], (n_pairs, LANES, LANES))
    return dpair, lanes(q_decay), lanes(k_decay), sdec


@jax.jit
def kernel(x, c, w_ada, b_ada, g_mix, w_in, conv_w, conv_b, q_norm, k_norm, sinks, ret_gn, w_out, g_ffn,
           w_gu, w_down):
    batch, seq, d = x.shape
    depth = w_ada.shape[0]
    assert batch == 1 and seq % ROW_TILE == 0
    ret_w = ret_gn.shape[1]
    swa_w = sinks.shape[1] * HEAD_DIM
    conv_c = conv_w.shape[2]
    kv_w = SWA_KV_HEADS * HEAD_DIM
    assert w_in.shape[2] == 4 * ret_w + swa_w + 2 * kv_w + 3 * conv_c
    assert ret_w % LANES == 0 and swa_w % LANES == 0 and kv_w == LANES and conv_c % LANES == 0
    assert w_down.shape[1] % FFN_COL == 0 and w_ada.shape[2] % ADA_COL == 0
    dims = (ret_w, swa_w, kv_w, conv_c)

    mod = _ada_modulation(c, w_ada, b_ada)[:, 0, :].reshape(depth, 6, 1, d)
    cs_t = _rope_table(seq)
    tables = _retention_tables(ret_w // HEAD_DIM)
    reps = LANES // HEAD_DIM
    convp = jnp.concatenate([conv_w, conv_b[:, None, :]], axis=1)
    qn = jnp.tile(q_norm, (1, reps))[:, None, :]
    kn = jnp.tile(k_norm, (1, reps))[:, None, :]

    f32_weights = (w_in, w_out, w_gu, w_down)
    weights = tuple(w[:1].astype(BF16) for w in f32_weights)
    xs = x[0]
    for l in range(depth):
        xs, weights = _layer(l, xs, sinks, mod, g_mix[:, None, :], weights, cs_t, convp, qn, kn,
                             ret_gn[:, None, :], tables, g_ffn[:, None, :],
                             f32_weights if l + 1 < depth else (), dims)
    return xs[None]
```

```python
import functools

import jax
import jax.numpy as jnp
from jax import lax
from jax.experimental import pallas as pl
from jax.experimental.pallas import tpu as pltpu

HEAD_DIM = 64
LANES = 128
BF16_SUBLANES = 16
SWA_KV_HEADS = 2
CONV_K = 3
CHUNK = 128
ROPE_THETA = 10000.0
EPS = 1e-6
NEG_INF = -1e30
ROW_TILE = 512
FFN_COL = 256
FILL_BEFORE_PROJ = 3
ADA_COL = 2048
VMEM_LIMIT_BYTES = 60 * 1024 * 1024

F32 = jnp.float32
BF16 = jnp.bfloat16


def _dot(a, b):
    return jnp.dot(a, b, preferred_element_type=F32)


def _dot_nt(a, b):
    return lax.dot_general(a, b, (((1,), (1,)), ((), ())), preferred_element_type=F32)


def _dot_tn(a, b):
    return lax.dot_general(a, b, (((0,), (0,)), ((), ())), preferred_element_type=F32)


def _lane_lo(shape):
    return lax.broadcasted_iota(jnp.int32, shape, 1) % LANES < HEAD_DIM


def _head_sum(a, lo):
    s_lo = jnp.sum(jnp.where(lo, a, 0.0), axis=-1, keepdims=True)
    s_hi = jnp.sum(jnp.where(lo, 0.0, a), axis=-1, keepdims=True)
    return jnp.where(lo, s_lo, s_hi)


def _rope(a, cos, sin_signed, first_half):
    width = a.shape[-1]
    fwd = pltpu.roll(a, width - HEAD_DIM // 2, 1)
    bwd = pltpu.roll(a, HEAD_DIM // 2, 1)
    return a * cos + jnp.where(first_half, fwd, bwd) * sin_signed


def _silu(a):
    return a * jax.nn.sigmoid(a)


def _modulated_rms_norm(x, gain_scale, shift):
    ms = jnp.mean(x * x, axis=-1, keepdims=True)
    return (x * lax.rsqrt(ms + EPS) * gain_scale + shift).astype(BF16)


def _ada_kernel(c_ref, w_ref, b_ref, o_ref):
    s = _silu(c_ref[...])
    w = w_ref[0]
    s_hi = s.astype(BF16)
    s_lo = (s - s_hi.astype(F32)).astype(BF16)
    w_hi = w.astype(BF16)
    w_lo = (w - w_hi.astype(F32)).astype(BF16)
    acc = _dot(s_hi, w_lo) + _dot(s_lo, w_hi) + _dot(s_hi, w_hi)
    o_ref[0] = acc + b_ref[0]


def _ada_modulation(c, w_ada, b_ada):
    depth, d, e = w_ada.shape
    c8 = jnp.broadcast_to(c, (8, d))
    return pl.pallas_call(
        _ada_kernel,
        grid=(depth, e // ADA_COL),
        in_specs=[
            pl.BlockSpec((8, d), lambda l, j: (0, 0)),
            pl.BlockSpec((1, d, ADA_COL), lambda l, j: (l, 0, j)),
            pl.BlockSpec((1, 1, ADA_COL), lambda l, j: (l, 0, j)),
        ],
        out_specs=pl.BlockSpec((1, 8, ADA_COL), lambda l, j: (l, 0, j)),
        out_shape=jax.ShapeDtypeStruct((depth, 8, e), F32),
        compiler_params=pltpu.CompilerParams(
            dimension_semantics=("arbitrary", "arbitrary"), vmem_limit_bytes=VMEM_LIMIT_BYTES),
        name="ada_modulation",
    )(c8, w_ada, b_ada.reshape(depth, 1, e))


def _mixer_tile(dims, layer, first_tile, x, mod_ref, sinks_ref, gmix_ref, win_ref, cs_ref, convp_ref,
                qn_ref, kn_ref, gn_ref, dpair_ref, qdec_ref, kdec_ref, sdec_ref, wout_ref,
                proj_s, mix_s, state_s, kprev_s, vprev_s, zbuf_s, fill):
    ret_w, swa_w, kv_w, conv_c = dims
    n_pairs = ret_w // LANES
    n_qcols = swa_w // LANES
    group = (swa_w // HEAD_DIM) // SWA_KV_HEADS
    tile = x.shape[0]

    off_rq, off_rk, off_rv, off_rg = 0, ret_w, 2 * ret_w, 3 * ret_w
    off_aq = 4 * ret_w
    off_ak = off_aq + swa_w
    off_av = off_ak + kv_w
    off_cb = off_av + kv_w
    off_cc = off_cb + conv_c
    off_cu = off_cc + conv_c

    shift1, scale1, gate1 = mod_ref[0, 0], mod_ref[0, 1], mod_ref[0, 2]
    h = _modulated_rms_norm(x, gmix_ref[0] * (1.0 + scale1), shift1)
    fill(FILL_BEFORE_PROJ)
    proj_s[...] = _dot(h, win_ref[0])

    sq = (CHUNK, LANES)
    lo = _lane_lo(sq)
    first_half = lax.broadcasted_iota(jnp.int32, sq, 1) % HEAD_DIM < HEAD_DIM // 2
    row = lax.broadcasted_iota(jnp.int32, sq, 0)
    col = lax.broadcasted_iota(jnp.int32, sq, 1)
    block_diag = (row < HEAD_DIM) == (col < HEAD_DIM)
    qi = lax.broadcasted_iota(jnp.int32, (CHUNK, 2 * CHUNK), 0)
    kj = lax.broadcasted_iota(jnp.int32, (CHUNK, 2 * CHUNK), 1)
    band = (kj > qi) & (kj <= qi + CHUNK)
    zero_b = jnp.zeros(sq, BF16)
    reps = LANES // (HEAD_DIM // 2)
    sin_sign = jnp.where(first_half, -1.0, 1.0)
    ones_b = jnp.ones((2 * CHUNK, LANES), BF16)

    def blk(off, c, width=LANES):
        return (pl.ds(c * CHUNK, CHUNK), pl.ds(off, width))

    states = [state_s[p] for p in range(n_pairs)]
    k_prev = [kprev_s[grp] for grp in range(SWA_KV_HEADS)]
    v_prev = [vprev_s[grp] for grp in range(SWA_KV_HEADS)]

    for c in range(tile // CHUNK):
        rows = pl.ds(c * CHUNK, CHUNK)
        cs = cs_ref[rows, :]
        cos = jnp.concatenate([cs[:, :HEAD_DIM // 2]] * reps, axis=1)
        sin = jnp.concatenate([cs[:, HEAD_DIM // 2:]] * reps, axis=1) * sin_sign

        qs, ks, vs, scores = [], [], [], []
        for p in range(n_pairs):
            q = _rope(proj_s[blk(off_rq + p * LANES, c)], cos, sin, first_half)
            k = _rope(proj_s[blk(off_rk + p * LANES, c)], cos, sin, first_half) * (HEAD_DIM ** -0.5)
            v = proj_s[blk(off_rv + p * LANES, c)].astype(BF16)
            kb = k.astype(BF16)
            k_bd = jnp.concatenate([jnp.where(lo, kb, zero_b), jnp.where(lo, zero_b, kb)], axis=0)
            scores.append(_dot_nt(q.astype(BF16), k_bd) * dpair_ref[p])
            qs.append(q)
            ks.append(k)
            vs.append(v)
        fill(1)
        for p in range(n_pairs):
            q, k, v = qs[p], ks[p], vs[p]
            g = proj_s[blk(off_rg + p * LANES, c)]
            v_bd = jnp.concatenate([jnp.where(lo, v, zero_b), jnp.where(lo, zero_b, v)], axis=0)
            inner = _dot(scores[p].astype(BF16), v_bd)
            cross = _dot((q * qdec_ref[p]).astype(BF16), states[p].astype(BF16))
            kv = _dot_tn((k * kdec_ref[p]).astype(BF16), v)
            states[p] = states[p] * sdec_ref[p] + jnp.where(block_diag, kv, 0.0)
            o = inner + cross
            mu = _head_sum(o, lo) * (1.0 / HEAD_DIM)
            d = o - mu
            var = _head_sum(d * d, lo) * (1.0 / HEAD_DIM)
            o = d * lax.rsqrt(var + EPS) * gn_ref[0, :, pl.ds(p * LANES, LANES)]
            mix_s[blk(p * LANES, c)] = (o * _silu(g)).astype(BF16)

        def head_norm(a, gain):
            ms = _head_sum(a * a, lo) * (1.0 / HEAD_DIM)
            return a * lax.rsqrt(ms + EPS) * gain

        k = _rope(head_norm(proj_s[blk(off_ak, c)], kn_ref[0]), cos, sin, first_half)
        v = proj_s[blk(off_av, c)]
        k_sw = pltpu.roll(k, HEAD_DIM, 1)
        v_sw = pltpu.roll(v, HEAD_DIM, 1)
        k_cur = [jnp.where(lo, k, k_sw).astype(BF16), jnp.where(lo, k_sw, k).astype(BF16)]
        v_cur = [jnp.where(lo, v, v_sw).astype(BF16), jnp.where(lo, v_sw, v).astype(BF16)]
        qcols = [
            _rope(head_norm(proj_s[blk(off_aq + j * LANES, c)], qn_ref[0]), cos, sin, first_half)
            * (HEAD_DIM ** -0.5)
            for j in range(n_qcols)
        ]
        if c == 0:
            first_key = jnp.where(first_tile, CHUNK, 0)
            mask = band & (kj >= first_key)
        else:
            mask = band
        heads_out, swa_scores = [], []
        for grp in range(SWA_KV_HEADS):
            heads = [grp * group + j for j in range(group)]
            q_st = jnp.concatenate(
                [jnp.where(lo if hd % 2 == 0 else ~lo, qcols[hd // 2], 0.0).astype(BF16) for hd in heads],
                axis=0)
            k_win = jnp.concatenate([k_prev[grp], k_cur[grp]], axis=0)
            swa_scores.append(_dot_nt(q_st, k_win))
        fill(1)
        for grp in range(SWA_KV_HEADS):
            heads = [grp * group + j for j in range(group)]
            v_win = jnp.concatenate(
                [jnp.concatenate([v_prev[grp], v_cur[grp]], axis=0), ones_b], axis=1)
            sc = swa_scores[grp]
            probs, maxes = [], []
            for j, hd in enumerate(heads):
                s_h = jnp.where(mask, sc[j * CHUNK:(j + 1) * CHUNK], NEG_INF)
                m = jnp.maximum(jnp.max(s_h, axis=-1, keepdims=True), sinks_ref[layer, hd])
                probs.append(jnp.exp(s_h - m).astype(BF16))
                maxes.append(m)
            ov = _dot(jnp.concatenate(probs, axis=0), v_win)
            for j, hd in enumerate(heads):
                o_h = ov[j * CHUNK:(j + 1) * CHUNK]
                denom = o_h[:, LANES:] + jnp.exp(sinks_ref[layer, hd] - maxes[j])
                heads_out.append(o_h[:, :LANES] / denom)
        for j in range(n_qcols):
            mix_s[blk(ret_w + j * LANES, c)] = jnp.where(lo, heads_out[2 * j], heads_out[2 * j + 1]).astype(BF16)
        k_prev, v_prev = k_cur, v_cur

        z = proj_s[blk(off_cc, c, conv_c)] * proj_s[blk(off_cu, c, conv_c)]
        zbuf_s[pl.ds(8 + c * CHUNK, CHUNK), :] = z
        z1 = zbuf_s[pl.ds(7 + c * CHUNK, CHUNK), :]
        z2 = zbuf_s[pl.ds(6 + c * CHUNK, CHUNK), :]
        y = (z2 * convp_ref[0, 0:1, :] + z1 * convp_ref[0, 1:2, :] + z * convp_ref[0, 2:3, :]
             + convp_ref[0, 3:4, :])
        mix_s[blk(ret_w + swa_w, c, conv_c)] = (proj_s[blk(off_cb, c, conv_c)] * y).astype(BF16)

    for p in range(n_pairs):
        state_s[p] = states[p]
    for grp in range(SWA_KV_HEADS):
        kprev_s[grp] = k_prev[grp]
        vprev_s[grp] = v_prev[grp]
    zbuf_s[0:8, :] = zbuf_s[tile:tile + 8, :]
    return x + gate1 * _dot(mix_s[...], wout_ref[0])


def _ffn_slab(j, h, wgu_ref, act_s):
    hidden = act_s.shape[1]
    cols = pl.ds(j * FFN_COL, FFN_COL)
    gt = _dot(h, wgu_ref[0, :, cols])
    up = _dot(h, wgu_ref[0, :, pl.ds(hidden + j * FFN_COL, FFN_COL)])
    act_s[:, cols] = (_silu(gt) * up).astype(BF16)


N_LAYER_INPUTS = 18


def _layer_kernel(dims, layer, n_cast, *refs):
    (sinks_ref, x_ref, mod_ref, gmix_ref, win_ref, cs_ref, convp_ref, qn_ref, kn_ref, gn_ref,
     dpair_ref, qdec_ref, kdec_ref, sdec_ref, wout_ref, gffn_ref, wgu_ref, wdown_ref) = refs[:N_LAYER_INPUTS]
    cast_in = refs[N_LAYER_INPUTS:N_LAYER_INPUTS + n_cast]
    o_ref = refs[N_LAYER_INPUTS + n_cast]
    cast_out = refs[N_LAYER_INPUTS + n_cast + 1:N_LAYER_INPUTS + 2 * n_cast + 1]
    (proj_s, mix_s, state_s, kprev_s, vprev_s, zbuf_s, xmid_s, xnew_s, act_s) = refs[N_LAYER_INPUTS + 2 * n_cast + 1:]
    cast_refs = tuple(zip(cast_in, cast_out))
    step = pl.program_id(0)

    @pl.when(step == 0)
    def _init():
        state_s[...] = jnp.zeros_like(state_s)
        kprev_s[...] = jnp.zeros_like(kprev_s)
        vprev_s[...] = jnp.zeros_like(vprev_s)
        zbuf_s[0:8, :] = jnp.zeros((8, zbuf_s.shape[1]), F32)
        xnew_s[...] = jnp.zeros_like(xnew_s)

    for src, dst in cast_refs:
        dst[...] = src[...].astype(BF16)

    xmid_s[...] = xnew_s[...]
    shift2, scale2, gate2 = mod_ref[0, 3], mod_ref[0, 4], mod_ref[0, 5]
    h2 = _modulated_rms_norm(xmid_s[...], gffn_ref[0] * (1.0 + scale2), shift2)
    slabs = iter(range(act_s.shape[1] // FFN_COL))

    def fill(n):
        for _ in range(n):
            j = next(slabs, None)
            if j is not None:
                _ffn_slab(j, h2, wgu_ref, act_s)

    xnew_s[...] = _mixer_tile(
        dims, layer, step == 0, x_ref[...], mod_ref, sinks_ref, gmix_ref, win_ref, cs_ref, convp_ref,
        qn_ref, kn_ref, gn_ref, dpair_ref, qdec_ref, kdec_ref, sdec_ref, wout_ref,
        proj_s, mix_s, state_s, kprev_s, vprev_s, zbuf_s, fill)
    fill(act_s.shape[1] // FFN_COL)
    o_ref[...] = xmid_s[...] + gate2 * _dot(act_s[...], wdown_ref[0])


def _fixed_block(block, index):
    return pl.BlockSpec(block, lambda i: index, pipeline_mode=pl.Buffered(1))


def _whole(arr):
    return _fixed_block(arr.shape, (0,) * arr.ndim)


def _layer_slice(arr, layer):
    return _fixed_block((1,) + arr.shape[1:], (layer,) + (0,) * (arr.ndim - 1))


def _cast_chunk_rows(rows, n_steps):
    return next(c for c in range(BF16_SUBLANES, rows + 1, BF16_SUBLANES)
                if rows % c == 0 and rows // c <= n_steps)


def _layer(layer, x, sinks, mod, g_mix, weights, cs_t, convp, qn, kn, gn, tables, g_ffn, next_f32, dims):
    seq, d = x.shape
    ret_w, swa_w, kv_w, conv_c = dims
    w_in, w_out, w_gu, w_down = weights
    in_w = w_in.shape[2]
    hidden = w_down.shape[1]
    n_pairs = ret_w // LANES
    dpair, qdec, kdec, sdec = tables
    tile = ROW_TILE
    n_tiles = seq // tile

    def mixer_rows(i):
        return (jnp.minimum(i, n_tiles - 1), 0)

    def ffn_rows(i):
        return (jnp.maximum(i - 1, 0), 0)

    def chunk_rows(i, which, last):
        return (which, jnp.minimum(i, last), 0)

    cast_in_specs, cast_out_specs, cast_out_shapes = [], [], []
    for w in next_f32:
        _, rows, cols = w.shape
        chunk = _cast_chunk_rows(rows, n_tiles)
        last = rows // chunk - 1
        cast_in_specs.append(
            pl.BlockSpec((1, chunk, cols), functools.partial(chunk_rows, which=layer + 1, last=last)))
        cast_out_specs.append(
            pl.BlockSpec((1, chunk, cols), functools.partial(chunk_rows, which=0, last=last)))
        cast_out_shapes.append(jax.ShapeDtypeStruct((1, rows, cols), BF16))

    outs = pl.pallas_call(
        functools.partial(_layer_kernel, dims, layer, len(next_f32)),
        grid=(n_tiles + 1,),
        in_specs=[
            pl.BlockSpec(memory_space=pltpu.SMEM),
            pl.BlockSpec((tile, d), mixer_rows),
            _layer_slice(mod, layer),
            _layer_slice(g_mix, layer),
            _whole(w_in),
            pl.BlockSpec((tile, HEAD_DIM), mixer_rows),
            _layer_slice(convp, layer),
            _layer_slice(qn, layer),
            _layer_slice(kn, layer),
            _layer_slice(gn, layer),
            _whole(dpair),
            _whole(qdec),
            _whole(kdec),
            _whole(sdec),
            _whole(w_out),
            _layer_slice(g_ffn, layer),
            _whole(w_gu),
            _whole(w_down),
        ] + cast_in_specs,
        out_specs=[pl.BlockSpec((tile, d), ffn_rows)] + cast_out_specs,
        out_shape=[jax.ShapeDtypeStruct((seq, d), F32)] + cast_out_shapes,
        scratch_shapes=[
            pltpu.VMEM((tile, in_w), F32),
            pltpu.VMEM((tile, d), BF16),
            pltpu.VMEM((n_pairs, LANES, LANES), F32),
            pltpu.VMEM((SWA_KV_HEADS, CHUNK, LANES), BF16),
            pltpu.VMEM((SWA_KV_HEADS, CHUNK, LANES), BF16),
            pltpu.VMEM((tile + 8, conv_c), F32),
            pltpu.VMEM((tile, d), F32),
            pltpu.VMEM((tile, d), F32),
            pltpu.VMEM((tile, hidden), BF16),
        ],
        compiler_params=pltpu.CompilerParams(
            dimension_semantics=("arbitrary",), vmem_limit_bytes=VMEM_LIMIT_BYTES),
        name="layer",
    )(sinks, x, mod, g_mix, w_in, cs_t, convp, qn, kn, gn, dpair, qdec, kdec, sdec, w_out,
      g_ffn, w_gu, w_down, *next_f32)
    return outs[0], tuple(outs[1:])


def _rope_table(seq):
    half = HEAD_DIM // 2
    per_row = LANES // half
    inv = 1.0 / (ROPE_THETA ** (jnp.arange(0, HEAD_DIM, 2, dtype=F32) / HEAD_DIM))
    lane = jnp.arange(LANES)
    pos = (jnp.arange(seq // per_row)[:, None] * per_row + (lane // half)[None, :]).astype(F32)
    ang = pos * inv[lane % half][None, :]
    cos, sin = lax.optimization_barrier((jnp.cos(ang), jnp.sin(ang)))
    return jnp.concatenate([cos.reshape(seq, half), sin.reshape(seq, half)], axis=1)


def _retention_tables(ret_heads):
    log_gamma = jnp.log(1.0 - 2.0 ** (-5.0 - jnp.arange(ret_heads, dtype=F32)))
    idx = jnp.arange(CHUNK, dtype=F32)
    rel = idx[:, None] - idx[None, :]
    intra = jnp.where(rel >= 0, jnp.exp(log_gamma[:, None, None] * jnp.maximum(rel, 0.0)), 0.0)
    k_decay = jnp.exp(log_gamma[:, None] * (CHUNK - 1.0 - idx)[None, :])
    q_decay = jnp.exp(log_gamma[:, None] * (idx + 1.0)[None, :])
    chunk_decay = jnp.exp(log_gamma * CHUNK)
    n_pairs = ret_heads // 2

    def lanes(per_head):
        wide = jnp.repeat(per_head[:, :, None], HEAD_DIM, axis=2)
        return jnp.concatenate([wide[0::2], wide[1::2]], axis=2)

    dpair = jnp.concatenate([intra[0::2], intra[1::2]], axis=2)
    sdec = jnp.broadcast_to(
        jnp.repeat(chunk_decay.reshape(n_pairs, 2), HEAD_DIM, axis=1)[:, :, None], (n_pairs, LANES, LANES))
    return dpair, lanes(q_decay), lanes(k_decay), sdec


@jax.jit
def kernel(x, c, w_ada, b_ada, g_mix, w_in, conv_w, conv_b, q_norm, k_norm, sinks, ret_gn, w_out, g_ffn,
           w_gu, w_down):
    batch, seq, d = x.shape
    depth = w_ada.shape[0]
    assert batch == 1 and seq % ROW_TILE == 0
    ret_w = ret_gn.shape[1]
    swa_w = sinks.shape[1] * HEAD_DIM
    conv_c = conv_w.shape[2]
    kv_w = SWA_KV_HEADS * HEAD_DIM
    assert w_in.shape[2] == 4 * ret_w + swa_w + 2 * kv_w + 3 * conv_c
    assert ret_w % LANES == 0 and swa_w % LANES == 0 and kv_w == LANES and conv_c % LANES == 0
    assert w_down.shape[1] % FFN_COL == 0 and w_ada.shape[2] % ADA_COL == 0
    dims = (ret_w, swa_w, kv_w, conv_c)

    mod = _ada_modulation(c, w_ada, b_ada)[:, 0, :].reshape(depth, 6, 1, d)
    cs_t = _rope_table(seq)
    tables = _retention_tables(ret_w // HEAD_DIM)
    reps = LANES // HEAD_DIM
    convp = jnp.concatenate([conv_w, conv_b[:, None, :]], axis=1)
    qn = jnp.tile(q_norm, (1, reps))[:, None, :]
    kn = jnp.tile(k_norm, (1, reps))[:, None, :]

    f32_weights = (w_in, w_out, w_gu, w_down)
    weights = tuple(w[:1].astype(BF16) for w in f32_weights)
    xs = x[0]
    for l in range(depth):
        xs, weights = _layer(l, xs, sinks, mod, g_mix[:, None, :], weights, cs_t, convp, qn, kn,
                             ret_gn[:, None, :], tables, g_ffn[:, None, :],
                             f32_weights if l + 1 < depth else (), dims)
    return xs[None]
```

```python
import functools

import jax
import jax.numpy as jnp
from jax import lax
from jax.experimental import pallas as pl
from jax.experimental.pallas import tpu as pltpu

HEAD_DIM = 64
LANES = 128
BF16_SUBLANES = 16
SWA_KV_HEADS = 2
CONV_K = 3
CHUNK = 128
ROPE_THETA = 10000.0
EPS = 1e-6
NEG_INF = -1e30
ROW_TILE = 512
FFN_COL = 256
FILL_BEFORE_PROJ = 3
ADA_COL = 2048
VMEM_LIMIT_BYTES = 60 * 1024 * 1024

F32 = jnp.float32
BF16 = jnp.bfloat16


def _dot(a, b):
    return jnp.dot(a, b, preferred_element_type=F32)


def _dot_nt(a, b):
    return lax.dot_general(a, b, (((1,), (1,)), ((), ())), preferred_element_type=F32)


def _dot_tn(a, b):
    return lax.dot_general(a, b, (((0,), (0,)), ((), ())), preferred_element_type=F32)


def _lane_lo(shape):
    return lax.broadcasted_iota(jnp.int32, shape, 1) % LANES < HEAD_DIM


def _head_sum(a, lo):
    s_lo = jnp.sum(jnp.where(lo, a, 0.0), axis=-1, keepdims=True)
    s_hi = jnp.sum(jnp.where(lo, 0.0, a), axis=-1, keepdims=True)
    return jnp.where(lo, s_lo, s_hi)


def _rope(a, cos, sin_signed, first_half):
    width = a.shape[-1]
    fwd = pltpu.roll(a, width - HEAD_DIM // 2, 1)
    bwd = pltpu.roll(a, HEAD_DIM // 2, 1)
    return a * cos + jnp.where(first_half, fwd, bwd) * sin_signed


def _silu(a):
    return a * jax.nn.sigmoid(a)


def _modulated_rms_norm(x, gain_scale, shift):
    ms = jnp.mean(x * x, axis=-1, keepdims=True)
    return (x * lax.rsqrt(ms + EPS) * gain_scale + shift).astype(BF16)


def _ada_kernel(c_ref, w_ref, b_ref, o_ref):
    s = _silu(c_ref[...])
    for j in range(o_ref.shape[2] // LANES):
        cols = pl.ds(j * LANES, LANES)
        o_ref[0, :, cols] = jnp.sum(w_ref[0, :, cols] * s, axis=0, keepdims=True) + b_ref[0, :, cols]


def _ada_modulation(c, w_ada, b_ada):
    depth, d, e = w_ada.shape
    c_lanes = jnp.broadcast_to(c.reshape(d, 1), (d, LANES))
    return pl.pallas_call(
        _ada_kernel,
        grid=(depth, e // ADA_COL),
        in_specs=[
            pl.BlockSpec((d, LANES), lambda l, j: (0, 0)),
            pl.BlockSpec((1, d, ADA_COL), lambda l, j: (l, 0, j)),
            pl.BlockSpec((1, 1, ADA_COL), lambda l, j: (l, 0, j)),
        ],
        out_specs=pl.BlockSpec((1, 1, ADA_COL), lambda l, j: (l, 0, j)),
        out_shape=jax.ShapeDtypeStruct((depth, 1, e), F32),
        compiler_params=pltpu.CompilerParams(
            dimension_semantics=("arbitrary", "arbitrary"), vmem_limit_bytes=VMEM_LIMIT_BYTES),
        name="ada_modulation",
    )(c_lanes, w_ada, b_ada.reshape(depth, 1, e))


def _mixer_tile(dims, layer, first_tile, x, mod_ref, sinks_ref, gmix_ref, win_ref, cs_ref, convp_ref,
                qn_ref, kn_ref, gn_ref, dpair_ref, qdec_ref, kdec_ref, sdec_ref, wout_ref,
                proj_s, mix_s, state_s, kprev_s, vprev_s, zbuf_s, fill):
    ret_w, swa_w, kv_w, conv_c = dims
    n_pairs = ret_w // LANES
    n_qcols = swa_w // LANES
    group = (swa_w // HEAD_DIM) // SWA_KV_HEADS
    tile = x.shape[0]

    off_rq, off_rk, off_rv, off_rg = 0, ret_w, 2 * ret_w, 3 * ret_w
    off_aq = 4 * ret_w
    off_ak = off_aq + swa_w
    off_av = off_ak + kv_w
    off_cb = off_av + kv_w
    off_cc = off_cb + conv_c
    off_cu = off_cc + conv_c

    shift1, scale1, gate1 = mod_ref[0, 0], mod_ref[0, 1], mod_ref[0, 2]
    h = _modulated_rms_norm(x, gmix_ref[0] * (1.0 + scale1), shift1)
    fill(FILL_BEFORE_PROJ)
    proj_s[...] = _dot(h, win_ref[0])

    sq = (CHUNK, LANES)
    lo = _lane_lo(sq)
    first_half = lax.broadcasted_iota(jnp.int32, sq, 1) % HEAD_DIM < HEAD_DIM // 2
    row = lax.broadcasted_iota(jnp.int32, sq, 0)
    col = lax.broadcasted_iota(jnp.int32, sq, 1)
    block_diag = (row < HEAD_DIM) == (col < HEAD_DIM)
    qi = lax.broadcasted_iota(jnp.int32, (CHUNK, 2 * CHUNK), 0)
    kj = lax.broadcasted_iota(jnp.int32, (CHUNK, 2 * CHUNK), 1)
    band = (kj > qi) & (kj <= qi + CHUNK)
    zero_b = jnp.zeros(sq, BF16)
    reps = LANES // (HEAD_DIM // 2)
    sin_sign = jnp.where(first_half, -1.0, 1.0)
    ones_b = jnp.ones((2 * CHUNK, LANES), BF16)

    def blk(off, c, width=LANES):
        return (pl.ds(c * CHUNK, CHUNK), pl.ds(off, width))

    states = [state_s[p] for p in range(n_pairs)]
    k_prev = [kprev_s[grp] for grp in range(SWA_KV_HEADS)]
    v_prev = [vprev_s[grp] for grp in range(SWA_KV_HEADS)]

    for c in range(tile // CHUNK):
        rows = pl.ds(c * CHUNK, CHUNK)
        cs = cs_ref[rows, :]
        cos = jnp.concatenate([cs[:, :HEAD_DIM // 2]] * reps, axis=1)
        sin = jnp.concatenate([cs[:, HEAD_DIM // 2:]] * reps, axis=1) * sin_sign

        qs, ks, vs, scores = [], [], [], []
        for p in range(n_pairs):
            q = _rope(proj_s[blk(off_rq + p * LANES, c)], cos, sin, first_half)
            k = _rope(proj_s[blk(off_rk + p * LANES, c)], cos, sin, first_half) * (HEAD_DIM ** -0.5)
            v = proj_s[blk(off_rv + p * LANES, c)].astype(BF16)
            kb = k.astype(BF16)
            k_bd = jnp.concatenate([jnp.where(lo, kb, zero_b), jnp.where(lo, zero_b, kb)], axis=0)
            scores.append(_dot_nt(q.astype(BF16), k_bd) * dpair_ref[p])
            qs.append(q)
            ks.append(k)
            vs.append(v)
        fill(1)
        for p in range(n_pairs):
            q, k, v = qs[p], ks[p], vs[p]
            g = proj_s[blk(off_rg + p * LANES, c)]
            v_bd = jnp.concatenate([jnp.where(lo, v, zero_b), jnp.where(lo, zero_b, v)], axis=0)
            inner = _dot(scores[p].astype(BF16), v_bd)
            cross = _dot((q * qdec_ref[p]).astype(BF16), states[p].astype(BF16))
            kv = _dot_tn((k * kdec_ref[p]).astype(BF16), v)
            states[p] = states[p] * sdec_ref[p] + jnp.where(block_diag, kv, 0.0)
            o = inner + cross
            mu = _head_sum(o, lo) * (1.0 / HEAD_DIM)
            d = o - mu
            var = _head_sum(d * d, lo) * (1.0 / HEAD_DIM)
            o = d * lax.rsqrt(var + EPS) * gn_ref[0, :, pl.ds(p * LANES, LANES)]
            mix_s[blk(p * LANES, c)] = (o * _silu(g)).astype(BF16)

        def head_norm(a, gain):
            ms = _head_sum(a * a, lo) * (1.0 / HEAD_DIM)
            return a * lax.rsqrt(ms + EPS) * gain

        k = _rope(head_norm(proj_s[blk(off_ak, c)], kn_ref[0]), cos, sin, first_half)
        v = proj_s[blk(off_av, c)]
        k_sw = pltpu.roll(k, HEAD_DIM, 1)
        v_sw = pltpu.roll(v, HEAD_DIM, 1)
        k_cur = [jnp.where(lo, k, k_sw).astype(BF16), jnp.where(lo, k_sw, k).astype(BF16)]
        v_cur = [jnp.where(lo, v, v_sw).astype(BF16), jnp.where(lo, v_sw, v).astype(BF16)]
        qcols = [
            _rope(head_norm(proj_s[blk(off_aq + j * LANES, c)], qn_ref[0]), cos, sin, first_half)
            * (HEAD_DIM ** -0.5)
            for j in range(n_qcols)
        ]
        if c == 0:
            first_key = jnp.where(first_tile, CHUNK, 0)
            mask = band & (kj >= first_key)
        else:
            mask = band
        heads_out, swa_scores = [], []
        for grp in range(SWA_KV_HEADS):
            heads = [grp * group + j for j in range(group)]
            q_st = jnp.concatenate(
                [jnp.where(lo if hd % 2 == 0 else ~lo, qcols[hd // 2], 0.0).astype(BF16) for hd in heads],
                axis=0)
            k_win = jnp.concatenate([k_prev[grp], k_cur[grp]], axis=0)
            swa_scores.append(_dot_nt(q_st, k_win))
        fill(1)
        for grp in range(SWA_KV_HEADS):
            heads = [grp * group + j for j in range(group)]
            v_win = jnp.concatenate(
                [jnp.concatenate([v_prev[grp], v_cur[grp]], axis=0), ones_b], axis=1)
            sc = swa_scores[grp]
            probs, maxes = [], []
            for j, hd in enumerate(heads):
                s_h = jnp.where(mask, sc[j * CHUNK:(j + 1) * CHUNK], NEG_INF)
                m = jnp.maximum(jnp.max(s_h, axis=-1, keepdims=True), sinks_ref[layer, hd])
                probs.append(jnp.exp(s_h - m).astype(BF16))
                maxes.append(m)
            ov = _dot(jnp.concatenate(probs, axis=0), v_win)
            for j, hd in enumerate(heads):
                o_h = ov[j * CHUNK:(j + 1) * CHUNK]
                denom = o_h[:, LANES:] + jnp.exp(sinks_ref[layer, hd] - maxes[j])
                heads_out.append(o_h[:, :LANES] / denom)
        for j in range(n_qcols):
            mix_s[blk(ret_w + j * LANES, c)] = jnp.where(lo, heads_out[2 * j], heads_out[2 * j + 1]).astype(BF16)
        k_prev, v_prev = k_cur, v_cur

        z = proj_s[blk(off_cc, c, conv_c)] * proj_s[blk(off_cu, c, conv_c)]
        zbuf_s[pl.ds(8 + c * CHUNK, CHUNK), :] = z
        z1 = zbuf_s[pl.ds(7 + c * CHUNK, CHUNK), :]
        z2 = zbuf_s[pl.ds(6 + c * CHUNK, CHUNK), :]
        y = (z2 * convp_ref[0, 0:1, :] + z1 * convp_ref[0, 1:2, :] + z * convp_ref[0, 2:3, :]
             + convp_ref[0, 3:4, :])
        mix_s[blk(ret_w + swa_w, c, conv_c)] = (proj_s[blk(off_cb, c, conv_c)] * y).astype(BF16)

    for p in range(n_pairs):
        state_s[p] = states[p]
    for grp in range(SWA_KV_HEADS):
        kprev_s[grp] = k_prev[grp]
        vprev_s[grp] = v_prev[grp]
    zbuf_s[0:8, :] = zbuf_s[tile:tile + 8, :]
    return x + gate1 * _dot(mix_s[...], wout_ref[0])


def _ffn_slab(j, h, wgu_ref, act_s):
    hidden = act_s.shape[1]
    cols = pl.ds(j * FFN_COL, FFN_COL)
    gt = _dot(h, wgu_ref[0, :, cols])
    up = _dot(h, wgu_ref[0, :, pl.ds(hidden + j * FFN_COL, FFN_COL)])
    act_s[:, cols] = (_silu(gt) * up).astype(BF16)


N_LAYER_INPUTS = 18


def _layer_kernel(dims, layer, n_cast, *refs):
    (sinks_ref, x_ref, mod_ref, gmix_ref, win_ref, cs_ref, convp_ref, qn_ref, kn_ref, gn_ref,
     dpair_ref, qdec_ref, kdec_ref, sdec_ref, wout_ref, gffn_ref, wgu_ref, wdown_ref) = refs[:N_LAYER_INPUTS]
    cast_in = refs[N_LAYER_INPUTS:N_LAYER_INPUTS + n_cast]
    o_ref = refs[N_LAYER_INPUTS + n_cast]
    cast_out = refs[N_LAYER_INPUTS + n_cast + 1:N_LAYER_INPUTS + 2 * n_cast + 1]
    (proj_s, mix_s, state_s, kprev_s, vprev_s, zbuf_s, xmid_s, xnew_s, act_s) = refs[N_LAYER_INPUTS + 2 * n_cast + 1:]
    cast_refs = tuple(zip(cast_in, cast_out))
    step = pl.program_id(0)

    @pl.when(step == 0)
    def _init():
        state_s[...] = jnp.zeros_like(state_s)
        kprev_s[...] = jnp.zeros_like(kprev_s)
        vprev_s[...] = jnp.zeros_like(vprev_s)
        zbuf_s[0:8, :] = jnp.zeros((8, zbuf_s.shape[1]), F32)
        xnew_s[...] = jnp.zeros_like(xnew_s)

    for src, dst in cast_refs:
        dst[...] = src[...].astype(BF16)

    xmid_s[...] = xnew_s[...]
    shift2, scale2, gate2 = mod_ref[0, 3], mod_ref[0, 4], mod_ref[0, 5]
    h2 = _modulated_rms_norm(xmid_s[...], gffn_ref[0] * (1.0 + scale2), shift2)
    slabs = iter(range(act_s.shape[1] // FFN_COL))

    def fill(n):
        for _ in range(n):
            j = next(slabs, None)
            if j is not None:
                _ffn_slab(j, h2, wgu_ref, act_s)

    xnew_s[...] = _mixer_tile(
        dims, layer, step == 0, x_ref[...], mod_ref, sinks_ref, gmix_ref, win_ref, cs_ref, convp_ref,
        qn_ref, kn_ref, gn_ref, dpair_ref, qdec_ref, kdec_ref, sdec_ref, wout_ref,
        proj_s, mix_s, state_s, kprev_s, vprev_s, zbuf_s, fill)
    fill(act_s.shape[1] // FFN_COL)
    o_ref[...] = xmid_s[...] + gate2 * _dot(act_s[...], wdown_ref[0])


def _fixed_block(block, index):
    return pl.BlockSpec(block, lambda i: index, pipeline_mode=pl.Buffered(1))


def _whole(arr):
    return _fixed_block(arr.shape, (0,) * arr.ndim)


def _layer_slice(arr, layer):
    return _fixed_block((1,) + arr.shape[1:], (layer,) + (0,) * (arr.ndim - 1))


def _cast_chunk_rows(rows, n_steps):
    return next(c for c in range(BF16_SUBLANES, rows + 1, BF16_SUBLANES)
                if rows % c == 0 and rows // c <= n_steps)


def _layer(layer, x, sinks, mod, g_mix, weights, cs_t, convp, qn, kn, gn, tables, g_ffn, next_f32, dims):
    seq, d = x.shape
    ret_w, swa_w, kv_w, conv_c = dims
    w_in, w_out, w_gu, w_down = weights
    in_w = w_in.shape[2]
    hidden = w_down.shape[1]
    n_pairs = ret_w // LANES
    dpair, qdec, kdec, sdec = tables
    tile = ROW_TILE
    n_tiles = seq // tile

    def mixer_rows(i):
        return (jnp.minimum(i, n_tiles - 1), 0)

    def ffn_rows(i):
        return (jnp.maximum(i - 1, 0), 0)

    def chunk_rows(i, which, last):
        return (which, jnp.minimum(i, last), 0)

    cast_in_specs, cast_out_specs, cast_out_shapes = [], [], []
    for w in next_f32:
        _, rows, cols = w.shape
        chunk = _cast_chunk_rows(rows, n_tiles)
        last = rows // chunk - 1
        cast_in_specs.append(
            pl.BlockSpec((1, chunk, cols), functools.partial(chunk_rows, which=layer + 1, last=last)))
        cast_out_specs.append(
            pl.BlockSpec((1, chunk, cols), functools.partial(chunk_rows, which=0, last=last)))
        cast_out_shapes.append(jax.ShapeDtypeStruct((1, rows, cols), BF16))

    outs = pl.pallas_call(
        functools.partial(_layer_kernel, dims, layer, len(next_f32)),
        grid=(n_tiles + 1,),
        in_specs=[
            pl.BlockSpec(memory_space=pltpu.SMEM),
            pl.BlockSpec((tile, d), mixer_rows),
            _layer_slice(mod, layer),
            _layer_slice(g_mix, layer),
            _whole(w_in),
            pl.BlockSpec((tile, HEAD_DIM), mixer_rows),
            _layer_slice(convp, layer),
            _layer_slice(qn, layer),
            _layer_slice(kn, layer),
            _layer_slice(gn, layer),
            _whole(dpair),
            _whole(qdec),
            _whole(kdec),
            _whole(sdec),
            _whole(w_out),
            _layer_slice(g_ffn, layer),
            _whole(w_gu),
            _whole(w_down),
        ] + cast_in_specs,
        out_specs=[pl.BlockSpec((tile, d), ffn_rows)] + cast_out_specs,
        out_shape=[jax.ShapeDtypeStruct((seq, d), F32)] + cast_out_shapes,
        scratch_shapes=[
            pltpu.VMEM((tile, in_w), F32),
            pltpu.VMEM((tile, d), BF16),
            pltpu.VMEM((n_pairs, LANES, LANES), F32),
            pltpu.VMEM((SWA_KV_HEADS, CHUNK, LANES), BF16),
            pltpu.VMEM((SWA_KV_HEADS, CHUNK, LANES), BF16),
            pltpu.VMEM((tile + 8, conv_c), F32),
            pltpu.VMEM((tile, d), F32),
            pltpu.VMEM((tile, d), F32),
            pltpu.VMEM((tile, hidden), BF16),
        ],
        compiler_params=pltpu.CompilerParams(
            dimension_semantics=("arbitrary",), vmem_limit_bytes=VMEM_LIMIT_BYTES),
        name="layer",
    )(sinks, x, mod, g_mix, w_in, cs_t, convp, qn, kn, gn, dpair, qdec, kdec, sdec, w_out,
      g_ffn, w_gu, w_down, *next_f32)
    return outs[0], tuple(outs[1:])


def _rope_table(seq):
    half = HEAD_DIM // 2
    per_row = LANES // half
    inv = 1.0 / (ROPE_THETA ** (jnp.arange(0, HEAD_DIM, 2, dtype=F32) / HEAD_DIM))
    lane = jnp.arange(LANES)
    pos = (jnp.arange(seq // per_row)[:, None] * per_row + (lane // half)[None, :]).astype(F32)
    ang = pos * inv[lane % half][None, :]
    cos, sin = lax.optimization_barrier((jnp.cos(ang), jnp.sin(ang)))
    return jnp.concatenate([cos.reshape(seq, half), sin.reshape(seq, half)], axis=1)


def _retention_tables(ret_heads):
    log_gamma = jnp.log(1.0 - 2.0 ** (-5.0 - jnp.arange(ret_heads, dtype=F32)))
    idx = jnp.arange(CHUNK, dtype=F32)
    rel = idx[:, None] - idx[None, :]
    intra = jnp.where(rel >= 0, jnp.exp(log_gamma[:, None, None] * jnp.maximum(rel, 0.0)), 0.0)
    k_decay = jnp.exp(log_gamma[:, None] * (CHUNK - 1.0 - idx)[None, :])
    q_decay = jnp.exp(log_gamma[:, None] * (idx + 1.0)[None, :])
    chunk_decay = jnp.exp(log_gamma * CHUNK)
    n_pairs = ret_heads // 2

    def lanes(per_head):
        wide = jnp.repeat(per_head[:, :, None], HEAD_DIM, axis=2)
        return jnp.concatenate([wide[0::2], wide[1::2]], axis=2)

    dpair = jnp.concatenate([intra[0::2], intra[1::2]], axis=2)
    sdec = jnp.broadcast_to(
        jnp.repeat(chunk_decay.reshape(n_pairs, 2), HEAD_DIM, axis=1)[:, :, None], (n_pairs, LANES, LANES))
    return dpair, lanes(q_decay), lanes(k_decay), sdec


@jax.jit
def kernel(x, c, w_ada, b_ada, g_mix, w_in, conv_w, conv_b, q_norm, k_norm, sinks, ret_gn, w_out, g_ffn,
           w_gu, w_down):
    batch, seq, d = x.shape
    depth = w_ada.shape[0]
    assert batch == 1 and seq % ROW_TILE == 0
    ret_w = ret_gn.shape[1]
    swa_w = sinks.shape[1] * HEAD_DIM
    conv_c = conv_w.shape[2]
    kv_w = SWA_KV_HEADS * HEAD_DIM
    assert w_in.shape[2] == 4 * ret_w + swa_w + 2 * kv_w + 3 * conv_c
    assert ret_w % LANES == 0 and swa_w % LANES == 0 and kv_w == LANES and conv_c % LANES == 0
    assert w_down.shape[1] % FFN_COL == 0 and w_ada.shape[2] % ADA_COL == 0
    dims = (ret_w, swa_w, kv_w, conv_c)

    mod = _ada_modulation(c, w_ada, b_ada).reshape(depth, 6, 1, d)
    cs_t = _rope_table(seq)
    tables = _retention_tables(ret_w // HEAD_DIM)
    reps = LANES // HEAD_DIM
    convp = jnp.concatenate([conv_w, conv_b[:, None, :]], axis=1)
    qn = jnp.tile(q_norm, (1, reps))[:, None, :]
    kn = jnp.tile(k_norm, (1, reps))[:, None, :]

    f32_weights = (w_in, w_out, w_gu, w_down)
    weights = tuple(w[:1].astype(BF16) for w in f32_weights)
    xs = x[0]
    for l in range(depth):
        xs, weights = _layer(l, xs, sinks, mod, g_mix[:, None, :], weights, cs_t, convp, qn, kn,
                             ret_gn[:, None, :], tables, g_ffn[:, None, :],
                             f32_weights if l + 1 < depth else (), dims)
    return xs[None]
```

```python
import functools

import jax
import jax.numpy as jnp
from jax import lax
from jax.experimental import pallas as pl
from jax.experimental.pallas import tpu as pltpu

HEAD_DIM = 64
LANES = 128
BF16_SUBLANES = 16
SWA_KV_HEADS = 2
CONV_K = 3
CHUNK = 128
ROPE_THETA = 10000.0
EPS = 1e-6
NEG_INF = -1e30
ROW_TILE = 512
FFN_COL = 256
FILL_BEFORE_PROJ = 3
ADA_COL = 2048
ADA_STEP_COL = 256
VMEM_LIMIT_BYTES = 60 * 1024 * 1024

F32 = jnp.float32
BF16 = jnp.bfloat16


def _dot(a, b):
    return jnp.dot(a, b, preferred_element_type=F32)


def _dot_nt(a, b):
    return lax.dot_general(a, b, (((1,), (1,)), ((), ())), preferred_element_type=F32)


def _dot_tn(a, b):
    return lax.dot_general(a, b, (((0,), (0,)), ((), ())), preferred_element_type=F32)


def _lane_lo(shape):
    return lax.broadcasted_iota(jnp.int32, shape, 1) % LANES < HEAD_DIM


def _head_sum(a, lo):
    s_lo = jnp.sum(jnp.where(lo, a, 0.0), axis=-1, keepdims=True)
    s_hi = jnp.sum(jnp.where(lo, 0.0, a), axis=-1, keepdims=True)
    return jnp.where(lo, s_lo, s_hi)


def _rope(a, cos, sin_signed, first_half):
    width = a.shape[-1]
    fwd = pltpu.roll(a, width - HEAD_DIM // 2, 1)
    bwd = pltpu.roll(a, HEAD_DIM // 2, 1)
    return a * cos + jnp.where(first_half, fwd, bwd) * sin_signed


def _silu(a):
    return a * jax.nn.sigmoid(a)


def _modulated_rms_norm(x, gain_scale, shift):
    ms = jnp.mean(x * x, axis=-1, keepdims=True)
    return (x * lax.rsqrt(ms + EPS) * gain_scale + shift).astype(BF16)


def _ada_kernel(c_ref, w_ref, b_ref, o_ref):
    s = _silu(c_ref[...])
    for j in range(o_ref.shape[2] // LANES):
        cols = pl.ds(j * LANES, LANES)
        o_ref[0, :, cols] = jnp.sum(w_ref[0, :, cols] * s, axis=0, keepdims=True) + b_ref[0, :, cols]


def _first_layer_modulation(c_lanes, w_ada, b_ada):
    _, d, e = w_ada.shape
    return pl.pallas_call(
        _ada_kernel,
        grid=(e // ADA_COL,),
        in_specs=[
            pl.BlockSpec((d, LANES), lambda j: (0, 0)),
            pl.BlockSpec((1, d, ADA_COL), lambda j: (0, 0, j)),
            pl.BlockSpec((1, 1, ADA_COL), lambda j: (0, 0, j)),
        ],
        out_specs=pl.BlockSpec((1, 1, ADA_COL), lambda j: (0, 0, j)),
        out_shape=jax.ShapeDtypeStruct((1, 1, e), F32),
        compiler_params=pltpu.CompilerParams(
            dimension_semantics=("arbitrary",), vmem_limit_bytes=VMEM_LIMIT_BYTES),
        name="ada_modulation",
    )(c_lanes, w_ada, b_ada)


def _mixer_tile(dims, layer, first_tile, x, mod_ref, sinks_ref, gmix_ref, win_ref, cs_ref, convp_ref,
                qn_ref, kn_ref, gn_ref, dpair_ref, qdec_ref, kdec_ref, sdec_ref, wout_ref,
                proj_s, mix_s, state_s, kprev_s, vprev_s, zbuf_s, fill):
    ret_w, swa_w, kv_w, conv_c = dims
    n_pairs = ret_w // LANES
    n_qcols = swa_w // LANES
    group = (swa_w // HEAD_DIM) // SWA_KV_HEADS
    tile = x.shape[0]

    off_rq, off_rk, off_rv, off_rg = 0, ret_w, 2 * ret_w, 3 * ret_w
    off_aq = 4 * ret_w
    off_ak = off_aq + swa_w
    off_av = off_ak + kv_w
    off_cb = off_av + kv_w
    off_cc = off_cb + conv_c
    off_cu = off_cc + conv_c

    shift1, scale1, gate1 = mod_ref[0, 0], mod_ref[0, 1], mod_ref[0, 2]
    h = _modulated_rms_norm(x, gmix_ref[0] * (1.0 + scale1), shift1)
    fill(FILL_BEFORE_PROJ)
    proj_s[...] = _dot(h, win_ref[0])

    sq = (CHUNK, LANES)
    lo = _lane_lo(sq)
    first_half = lax.broadcasted_iota(jnp.int32, sq, 1) % HEAD_DIM < HEAD_DIM // 2
    row = lax.broadcasted_iota(jnp.int32, sq, 0)
    col = lax.broadcasted_iota(jnp.int32, sq, 1)
    block_diag = (row < HEAD_DIM) == (col < HEAD_DIM)
    qi = lax.broadcasted_iota(jnp.int32, (CHUNK, 2 * CHUNK), 0)
    kj = lax.broadcasted_iota(jnp.int32, (CHUNK, 2 * CHUNK), 1)
    band = (kj > qi) & (kj <= qi + CHUNK)
    zero_b = jnp.zeros(sq, BF16)
    reps = LANES // (HEAD_DIM // 2)
    sin_sign = jnp.where(first_half, -1.0, 1.0)
    ones_b = jnp.ones((2 * CHUNK, LANES), BF16)

    def blk(off, c, width=LANES):
        return (pl.ds(c * CHUNK, CHUNK), pl.ds(off, width))

    states = [state_s[p] for p in range(n_pairs)]
    k_prev = [kprev_s[grp] for grp in range(SWA_KV_HEADS)]
    v_prev = [vprev_s[grp] for grp in range(SWA_KV_HEADS)]

    for c in range(tile // CHUNK):
        rows = pl.ds(c * CHUNK, CHUNK)
        cs = cs_ref[rows, :]
        cos = jnp.concatenate([cs[:, :HEAD_DIM // 2]] * reps, axis=1)
        sin = jnp.concatenate([cs[:, HEAD_DIM // 2:]] * reps, axis=1) * sin_sign

        qs, ks, vs, scores = [], [], [], []
        for p in range(n_pairs):
            q = _rope(proj_s[blk(off_rq + p * LANES, c)], cos, sin, first_half)
            k = _rope(proj_s[blk(off_rk + p * LANES, c)], cos, sin, first_half) * (HEAD_DIM ** -0.5)
            v = proj_s[blk(off_rv + p * LANES, c)].astype(BF16)
            kb = k.astype(BF16)
            k_bd = jnp.concatenate([jnp.where(lo, kb, zero_b), jnp.where(lo, zero_b, kb)], axis=0)
            scores.append(_dot_nt(q.astype(BF16), k_bd) * dpair_ref[p])
            qs.append(q)
            ks.append(k)
            vs.append(v)
        fill(1)
        for p in range(n_pairs):
            q, k, v = qs[p], ks[p], vs[p]
            g = proj_s[blk(off_rg + p * LANES, c)]
            v_bd = jnp.concatenate([jnp.where(lo, v, zero_b), jnp.where(lo, zero_b, v)], axis=0)
            inner = _dot(scores[p].astype(BF16), v_bd)
            cross = _dot((q * qdec_ref[p]).astype(BF16), states[p].astype(BF16))
            kv = _dot_tn((k * kdec_ref[p]).astype(BF16), v)
            states[p] = states[p] * sdec_ref[p] + jnp.where(block_diag, kv, 0.0)
            o = inner + cross
            mu = _head_sum(o, lo) * (1.0 / HEAD_DIM)
            d = o - mu
            var = _head_sum(d * d, lo) * (1.0 / HEAD_DIM)
            o = d * lax.rsqrt(var + EPS) * gn_ref[0, :, pl.ds(p * LANES, LANES)]
            mix_s[blk(p * LANES, c)] = (o * _silu(g)).astype(BF16)

        def head_norm(a, gain):
            ms = _head_sum(a * a, lo) * (1.0 / HEAD_DIM)
            return a * lax.rsqrt(ms + EPS) * gain

        k = _rope(head_norm(proj_s[blk(off_ak, c)], kn_ref[0]), cos, sin, first_half)
        v = proj_s[blk(off_av, c)]
        k_sw = pltpu.roll(k, HEAD_DIM, 1)
        v_sw = pltpu.roll(v, HEAD_DIM, 1)
        k_cur = [jnp.where(lo, k, k_sw).astype(BF16), jnp.where(lo, k_sw, k).astype(BF16)]
        v_cur = [jnp.where(lo, v, v_sw).astype(BF16), jnp.where(lo, v_sw, v).astype(BF16)]
        qcols = [
            _rope(head_norm(proj_s[blk(off_aq + j * LANES, c)], qn_ref[0]), cos, sin, first_half)
            * (HEAD_DIM ** -0.5)
            for j in range(n_qcols)
        ]
        if c == 0:
            first_key = jnp.where(first_tile, CHUNK, 0)
            mask = band & (kj >= first_key)
        else:
            mask = band
        heads_out, swa_scores = [], []
        for grp in range(SWA_KV_HEADS):
            heads = [grp * group + j for j in range(group)]
            q_st = jnp.concatenate(
                [jnp.where(lo if hd % 2 == 0 else ~lo, qcols[hd // 2], 0.0).astype(BF16) for hd in heads],
                axis=0)
            k_win = jnp.concatenate([k_prev[grp], k_cur[grp]], axis=0)
            swa_scores.append(_dot_nt(q_st, k_win))
        fill(1)
        for grp in range(SWA_KV_HEADS):
            heads = [grp * group + j for j in range(group)]
            v_win = jnp.concatenate(
                [jnp.concatenate([v_prev[grp], v_cur[grp]], axis=0), ones_b], axis=1)
            sc = swa_scores[grp]
            probs, maxes = [], []
            for j, hd in enumerate(heads):
                s_h = jnp.where(mask, sc[j * CHUNK:(j + 1) * CHUNK], NEG_INF)
                m = jnp.maximum(jnp.max(s_h, axis=-1, keepdims=True), sinks_ref[layer, hd])
                probs.append(jnp.exp(s_h - m).astype(BF16))
                maxes.append(m)
            ov = _dot(jnp.concatenate(probs, axis=0), v_win)
            for j, hd in enumerate(heads):
                o_h = ov[j * CHUNK:(j + 1) * CHUNK]
                denom = o_h[:, LANES:] + jnp.exp(sinks_ref[layer, hd] - maxes[j])
                heads_out.append(o_h[:, :LANES] / denom)
        for j in range(n_qcols):
            mix_s[blk(ret_w + j * LANES, c)] = jnp.where(lo, heads_out[2 * j], heads_out[2 * j + 1]).astype(BF16)
        k_prev, v_prev = k_cur, v_cur

        z = proj_s[blk(off_cc, c, conv_c)] * proj_s[blk(off_cu, c, conv_c)]
        zbuf_s[pl.ds(8 + c * CHUNK, CHUNK), :] = z
        z1 = zbuf_s[pl.ds(7 + c * CHUNK, CHUNK), :]
        z2 = zbuf_s[pl.ds(6 + c * CHUNK, CHUNK), :]
        y = (z2 * convp_ref[0, 0:1, :] + z1 * convp_ref[0, 1:2, :] + z * convp_ref[0, 2:3, :]
             + convp_ref[0, 3:4, :])
        mix_s[blk(ret_w + swa_w, c, conv_c)] = (proj_s[blk(off_cb, c, conv_c)] * y).astype(BF16)

    for p in range(n_pairs):
        state_s[p] = states[p]
    for grp in range(SWA_KV_HEADS):
        kprev_s[grp] = k_prev[grp]
        vprev_s[grp] = v_prev[grp]
    zbuf_s[0:8, :] = zbuf_s[tile:tile + 8, :]
    return x + gate1 * _dot(mix_s[...], wout_ref[0])


def _ffn_slab(j, h, wgu_ref, act_s):
    hidden = act_s.shape[1]
    cols = pl.ds(j * FFN_COL, FFN_COL)
    gt = _dot(h, wgu_ref[0, :, cols])
    up = _dot(h, wgu_ref[0, :, pl.ds(hidden + j * FFN_COL, FFN_COL)])
    act_s[:, cols] = (_silu(gt) * up).astype(BF16)


N_LAYER_INPUTS = 18
N_ADA_INPUTS = 3


def _layer_kernel(dims, layer, n_cast, *refs):
    n_side_in = n_cast + N_ADA_INPUTS if n_cast else 0
    n_side_out = n_cast + 1 if n_cast else 0
    (sinks_ref, x_ref, mod_ref, gmix_ref, win_ref, cs_ref, convp_ref, qn_ref, kn_ref, gn_ref,
     dpair_ref, qdec_ref, kdec_ref, sdec_ref, wout_ref, gffn_ref, wgu_ref, wdown_ref) = refs[:N_LAYER_INPUTS]
    side_in = refs[N_LAYER_INPUTS:N_LAYER_INPUTS + n_side_in]
    o_ref = refs[N_LAYER_INPUTS + n_side_in]
    side_out = refs[N_LAYER_INPUTS + n_side_in + 1:N_LAYER_INPUTS + n_side_in + 1 + n_side_out]
    (proj_s, mix_s, state_s, kprev_s, vprev_s, zbuf_s, xmid_s, xnew_s, act_s) = (
        refs[N_LAYER_INPUTS + n_side_in + 1 + n_side_out:])
    cast_refs = tuple(zip(side_in[:n_cast], side_out[:n_cast]))
    step = pl.program_id(0)

    @pl.when(step == 0)
    def _init():
        state_s[...] = jnp.zeros_like(state_s)
        kprev_s[...] = jnp.zeros_like(kprev_s)
        vprev_s[...] = jnp.zeros_like(vprev_s)
        zbuf_s[0:8, :] = jnp.zeros((8, zbuf_s.shape[1]), F32)
        xnew_s[...] = jnp.zeros_like(xnew_s)

    for src, dst in cast_refs:
        dst[...] = src[...].astype(BF16)
    if n_cast:
        _ada_kernel(*side_in[n_cast:], side_out[n_cast])

    xmid_s[...] = xnew_s[...]
    shift2, scale2, gate2 = mod_ref[0, 3], mod_ref[0, 4], mod_ref[0, 5]
    h2 = _modulated_rms_norm(xmid_s[...], gffn_ref[0] * (1.0 + scale2), shift2)
    slabs = iter(range(act_s.shape[1] // FFN_COL))

    def fill(n):
        for _ in range(n):
            j = next(slabs, None)
            if j is not None:
                _ffn_slab(j, h2, wgu_ref, act_s)

    xnew_s[...] = _mixer_tile(
        dims, layer, step == 0, x_ref[...], mod_ref, sinks_ref, gmix_ref, win_ref, cs_ref, convp_ref,
        qn_ref, kn_ref, gn_ref, dpair_ref, qdec_ref, kdec_ref, sdec_ref, wout_ref,
        proj_s, mix_s, state_s, kprev_s, vprev_s, zbuf_s, fill)
    fill(act_s.shape[1] // FFN_COL)
    o_ref[...] = xmid_s[...] + gate2 * _dot(act_s[...], wdown_ref[0])


def _fixed_block(block, index):
    return pl.BlockSpec(block, lambda i: index, pipeline_mode=pl.Buffered(1))


def _whole(arr):
    return _fixed_block(arr.shape, (0,) * arr.ndim)


def _layer_slice(arr, layer):
    return _fixed_block((1,) + arr.shape[1:], (layer,) + (0,) * (arr.ndim - 1))


def _cast_chunk_rows(rows, n_steps):
    return next(c for c in range(BF16_SUBLANES, rows + 1, BF16_SUBLANES)
                if rows % c == 0 and rows // c <= n_steps)


def _layer(layer, x, sinks, mod, g_mix, weights, cs_t, convp, qn, kn, gn, tables, g_ffn, next_f32, ada, dims):
    seq, d = x.shape
    ret_w, swa_w, kv_w, conv_c = dims
    w_in, w_out, w_gu, w_down = weights
    in_w = w_in.shape[2]
    hidden = w_down.shape[1]
    n_pairs = ret_w // LANES
    dpair, qdec, kdec, sdec = tables
    tile = ROW_TILE
    n_tiles = seq // tile

    def mixer_rows(i):
        return (jnp.minimum(i, n_tiles - 1), 0)

    def ffn_rows(i):
        return (jnp.maximum(i - 1, 0), 0)

    def chunk_rows(i, which, last):
        return (which, jnp.minimum(i, last), 0)

    cast_in_specs, cast_out_specs, cast_out_shapes = [], [], []
    for w in next_f32:
        _, rows, cols = w.shape
        chunk = _cast_chunk_rows(rows, n_tiles)
        last = rows // chunk - 1
        cast_in_specs.append(
            pl.BlockSpec((1, chunk, cols), functools.partial(chunk_rows, which=layer + 1, last=last)))
        cast_out_specs.append(
            pl.BlockSpec((1, chunk, cols), functools.partial(chunk_rows, which=0, last=last)))
        cast_out_shapes.append(jax.ShapeDtypeStruct((1, rows, cols), BF16))
    side_inputs = tuple(next_f32)
    if next_f32:
        c_lanes, w_ada, b_ada = ada
        e = w_ada.shape[2]
        last = e // ADA_STEP_COL - 1
        assert last < n_tiles
        cast_in_specs += [
            _whole(c_lanes),
            pl.BlockSpec((1, d, ADA_STEP_COL), lambda i: (layer + 1, 0, jnp.minimum(i, last))),
            pl.BlockSpec((1, 1, ADA_STEP_COL), lambda i: (layer + 1, 0, jnp.minimum(i, last))),
        ]
        cast_out_specs.append(pl.BlockSpec((1, 1, ADA_STEP_COL), lambda i: (0, 0, jnp.minimum(i, last))))
        cast_out_shapes.append(jax.ShapeDtypeStruct((1, 1, e), F32))
        side_inputs += (c_lanes, w_ada, b_ada)

    outs = pl.pallas_call(
        functools.partial(_layer_kernel, dims, layer, len(next_f32)),
        grid=(n_tiles + 1,),
        in_specs=[
            pl.BlockSpec(memory_space=pltpu.SMEM),
            pl.BlockSpec((tile, d), mixer_rows),
            _whole(mod),
            _layer_slice(g_mix, layer),
            _whole(w_in),
            pl.BlockSpec((tile, HEAD_DIM), mixer_rows),
            _layer_slice(convp, layer),
            _layer_slice(qn, layer),
            _layer_slice(kn, layer),
            _layer_slice(gn, layer),
            _whole(dpair),
            _whole(qdec),
            _whole(kdec),
            _whole(sdec),
            _whole(w_out),
            _layer_slice(g_ffn, layer),
            _whole(w_gu),
            _whole(w_down),
        ] + cast_in_specs,
        out_specs=[pl.BlockSpec((tile, d), ffn_rows)] + cast_out_specs,
        out_shape=[jax.ShapeDtypeStruct((seq, d), F32)] + cast_out_shapes,
        scratch_shapes=[
            pltpu.VMEM((tile, in_w), F32),
            pltpu.VMEM((tile, d), BF16),
            pltpu.VMEM((n_pairs, LANES, LANES), F32),
            pltpu.VMEM((SWA_KV_HEADS, CHUNK, LANES), BF16),
            pltpu.VMEM((SWA_KV_HEADS, CHUNK, LANES), BF16),
            pltpu.VMEM((tile + 8, conv_c), F32),
            pltpu.VMEM((tile, d), F32),
            pltpu.VMEM((tile, d), F32),
            pltpu.VMEM((tile, hidden), BF16),
        ],
        compiler_params=pltpu.CompilerParams(
            dimension_semantics=("arbitrary",), vmem_limit_bytes=VMEM_LIMIT_BYTES),
        name="layer",
    )(sinks, x, mod, g_mix, w_in, cs_t, convp, qn, kn, gn, dpair, qdec, kdec, sdec, w_out,
      g_ffn, w_gu, w_down, *side_inputs)
    if not next_f32:
        return outs[0], (), None
    return outs[0], tuple(outs[1:-1]), outs[-1].reshape(mod.shape)


def _rope_table(seq):
    half = HEAD_DIM // 2
    per_row = LANES // half
    inv = 1.0 / (ROPE_THETA ** (jnp.arange(0, HEAD_DIM, 2, dtype=F32) / HEAD_DIM))
    lane = jnp.arange(LANES)
    pos = (jnp.arange(seq // per_row)[:, None] * per_row + (lane // half)[None, :]).astype(F32)
    ang = pos * inv[lane % half][None, :]
    cos, sin = lax.optimization_barrier((jnp.cos(ang), jnp.sin(ang)))
    return jnp.concatenate([cos.reshape(seq, half), sin.reshape(seq, half)], axis=1)


def _retention_tables(ret_heads):
    log_gamma = jnp.log(1.0 - 2.0 ** (-5.0 - jnp.arange(ret_heads, dtype=F32)))
    idx = jnp.arange(CHUNK, dtype=F32)
    rel = idx[:, None] - idx[None, :]
    intra = jnp.where(rel >= 0, jnp.exp(log_gamma[:, None, None] * jnp.maximum(rel, 0.0)), 0.0)
    k_decay = jnp.exp(log_gamma[:, None] * (CHUNK - 1.0 - idx)[None, :])
    q_decay = jnp.exp(log_gamma[:, None] * (idx + 1.0)[None, :])
    chunk_decay = jnp.exp(log_gamma * CHUNK)
    n_pairs = ret_heads // 2

    def lanes(per_head):
        wide = jnp.repeat(per_head[:, :, None], HEAD_DIM, axis=2)
        return jnp.concatenate([wide[0::2], wide[1::2]], axis=2)

    dpair = jnp.concatenate([intra[0::2], intra[1::2]], axis=2)
    sdec = jnp.broadcast_to(
        jnp.repeat(chunk_decay.reshape(n_pairs, 2), HEAD_DIM, axis=1)[:, :, None], (n_pairs, LANES, LANES))
    return dpair, lanes(q_decay), lanes(k_decay), sdec


@jax.jit
def kernel(x, c, w_ada, b_ada, g_mix, w_in, conv_w, conv_b, q_norm, k_norm, sinks, ret_gn, w_out, g_ffn,
           w_gu, w_down):
    batch, seq, d = x.shape
    depth = w_ada.shape[0]
    assert batch == 1 and seq % ROW_TILE == 0
    ret_w = ret_gn.shape[1]
    swa_w = sinks.shape[1] * HEAD_DIM
    conv_c = conv_w.shape[2]
    kv_w = SWA_KV_HEADS * HEAD_DIM
    assert w_in.shape[2] == 4 * ret_w + swa_w + 2 * kv_w + 3 * conv_c
    assert ret_w % LANES == 0 and swa_w % LANES == 0 and kv_w == LANES and conv_c % LANES == 0
    assert w_down.shape[1] % FFN_COL == 0 and w_ada.shape[2] % ADA_COL == 0 and w_ada.shape[2] % ADA_STEP_COL == 0
    dims = (ret_w, swa_w, kv_w, conv_c)

    c_lanes = jnp.broadcast_to(c.reshape(d, 1), (d, LANES))
    b_ada3 = b_ada[:, None, :]
    mod = _first_layer_modulation(c_lanes, w_ada, b_ada3).reshape(1, 6, 1, d)
    cs_t = _rope_table(seq)
    tables = _retention_tables(ret_w // HEAD_DIM)
    reps = LANES // HEAD_DIM
    convp = jnp.concatenate([conv_w, conv_b[:, None, :]], axis=1)
    qn = jnp.tile(q_norm, (1, reps))[:, None, :]
    kn = jnp.tile(k_norm, (1, reps))[:, None, :]

    f32_weights = (w_in, w_out, w_gu, w_down)
    weights = tuple(w[:1].astype(BF16) for w in f32_weights)
    xs = x[0]
    for l in range(depth):
        xs, weights, mod = _layer(l, xs, sinks, mod, g_mix[:, None, :], weights, cs_t, convp, qn, kn,
                                  ret_gn[:, None, :], tables, g_ffn[:, None, :],
                                  f32_weights if l + 1 < depth else (), (c_lanes, w_ada, b_ada3), dims)
    return xs[None]
```

```python
import functools

import jax
import jax.numpy as jnp
from jax import lax
from jax.experimental import pallas as pl
from jax.experimental.pallas import tpu as pltpu

HEAD_DIM = 64
LANES = 128
BF16_SUBLANES = 16
SWA_KV_HEADS = 2
CONV_K = 3
CONV_HIST = 8
CHUNK = 128
ROPE_THETA = 10000.0
EPS = 1e-6
NEG_INF = -1e30
ROW_TILE = 512
FFN_COL = 256
FILL_BEFORE_PROJ = 3
ADA_COL = 2048
VMEM_LIMIT_BYTES = 60 * 1024 * 1024

F32 = jnp.float32
BF16 = jnp.bfloat16


def _dot(a, b):
    return jnp.dot(a, b, preferred_element_type=F32)


def _dot_nt(a, b):
    return lax.dot_general(a, b, (((1,), (1,)), ((), ())), preferred_element_type=F32)


def _dot_tn(a, b):
    return lax.dot_general(a, b, (((0,), (0,)), ((), ())), preferred_element_type=F32)


def _lane_lo(shape):
    return lax.broadcasted_iota(jnp.int32, shape, 1) % LANES < HEAD_DIM


def _head_sum(a, lo):
    s_lo = jnp.sum(jnp.where(lo, a, 0.0), axis=-1, keepdims=True)
    s_hi = jnp.sum(jnp.where(lo, 0.0, a), axis=-1, keepdims=True)
    return jnp.where(lo, s_lo, s_hi)


def _rope(a, cos, sin_signed, first_half):
    width = a.shape[-1]
    fwd = pltpu.roll(a, width - HEAD_DIM // 2, 1)
    bwd = pltpu.roll(a, HEAD_DIM // 2, 1)
    return a * cos + jnp.where(first_half, fwd, bwd) * sin_signed


def _silu(a):
    return a * jax.nn.sigmoid(a)


def _modulated_rms_norm(x, gain_scale, shift):
    ms = jnp.mean(x * x, axis=-1, keepdims=True)
    return (x * lax.rsqrt(ms + EPS) * gain_scale + shift).astype(BF16)


def _ada_kernel(c_ref, w_ref, b_ref, o_ref):
    s = _silu(c_ref[...])
    for j in range(o_ref.shape[2] // LANES):
        cols = pl.ds(j * LANES, LANES)
        o_ref[0, :, cols] = jnp.sum(w_ref[0, :, cols] * s, axis=0, keepdims=True) + b_ref[0, :, cols]


def _ada_modulation(c, w_ada, b_ada):
    depth, d, e = w_ada.shape
    c_lanes = jnp.broadcast_to(c.reshape(d, 1), (d, LANES))
    return pl.pallas_call(
        _ada_kernel,
        grid=(depth, e // ADA_COL),
        in_specs=[
            pl.BlockSpec((d, LANES), lambda l, j: (0, 0)),
            pl.BlockSpec((1, d, ADA_COL), lambda l, j: (l, 0, j)),
            pl.BlockSpec((1, 1, ADA_COL), lambda l, j: (l, 0, j)),
        ],
        out_specs=pl.BlockSpec((1, 1, ADA_COL), lambda l, j: (l, 0, j)),
        out_shape=jax.ShapeDtypeStruct((depth, 1, e), F32),
        compiler_params=pltpu.CompilerParams(
            dimension_semantics=("arbitrary", "arbitrary"), vmem_limit_bytes=VMEM_LIMIT_BYTES),
        name="ada_modulation",
    )(c_lanes, w_ada, b_ada.reshape(depth, 1, e))


def _mixer_tile(dims, layer, first_tile, x, mod_ref, sinks_ref, gmix_ref, win_ref, cs_ref, convp_ref,
                qn_ref, kn_ref, gn_ref, dpair_ref, qdec_ref, kdec_ref, sdec_ref, wout_ref,
                proj_s, mix_s, state_s, kprev_s, vprev_s, zbuf_s, fill):
    ret_w, swa_w, kv_w, conv_c = dims
    n_pairs = ret_w // LANES
    n_qcols = swa_w // LANES
    group = (swa_w // HEAD_DIM) // SWA_KV_HEADS
    tile = x.shape[0]

    off_rq, off_rk, off_rv, off_rg = 0, ret_w, 2 * ret_w, 3 * ret_w
    off_aq = 4 * ret_w
    off_ak = off_aq + swa_w
    off_av = off_ak + kv_w
    off_cb = off_av + kv_w
    off_cc = off_cb + conv_c
    off_cu = off_cc + conv_c

    shift1, scale1, gate1 = mod_ref[0, 0], mod_ref[0, 1], mod_ref[0, 2]
    h = _modulated_rms_norm(x, gmix_ref[0] * (1.0 + scale1), shift1)
    fill(FILL_BEFORE_PROJ)
    proj_s[...] = _dot(h, win_ref[0])

    sq = (CHUNK, LANES)
    lo = _lane_lo(sq)
    first_half = lax.broadcasted_iota(jnp.int32, sq, 1) % HEAD_DIM < HEAD_DIM // 2
    row = lax.broadcasted_iota(jnp.int32, sq, 0)
    col = lax.broadcasted_iota(jnp.int32, sq, 1)
    block_diag = (row < HEAD_DIM) == (col < HEAD_DIM)
    qi = lax.broadcasted_iota(jnp.int32, (CHUNK, 2 * CHUNK), 0)
    kj = lax.broadcasted_iota(jnp.int32, (CHUNK, 2 * CHUNK), 1)
    band = (kj > qi) & (kj <= qi + CHUNK)
    zero_b = jnp.zeros(sq, BF16)
    reps = LANES // (HEAD_DIM // 2)
    sin_sign = jnp.where(first_half, -1.0, 1.0)
    ones_b = jnp.ones((2 * CHUNK, LANES), BF16)

    def blk(off, c, width=LANES):
        return (pl.ds(c * CHUNK, CHUNK), pl.ds(off, width))

    states = [state_s[p] for p in range(n_pairs)]
    k_prev = [kprev_s[grp] for grp in range(SWA_KV_HEADS)]
    v_prev = [vprev_s[grp] for grp in range(SWA_KV_HEADS)]

    for c in range(tile // CHUNK):
        rows = pl.ds(c * CHUNK, CHUNK)
        cs = cs_ref[rows, :]
        cos = jnp.concatenate([cs[:, :HEAD_DIM // 2]] * reps, axis=1)
        sin = jnp.concatenate([cs[:, HEAD_DIM // 2:]] * reps, axis=1) * sin_sign

        qs, ks, vs, scores = [], [], [], []
        for p in range(n_pairs):
            q = _rope(proj_s[blk(off_rq + p * LANES, c)], cos, sin, first_half)
            k = _rope(proj_s[blk(off_rk + p * LANES, c)], cos, sin, first_half) * (HEAD_DIM ** -0.5)
            v = proj_s[blk(off_rv + p * LANES, c)].astype(BF16)
            kb = k.astype(BF16)
            k_bd = jnp.concatenate([jnp.where(lo, kb, zero_b), jnp.where(lo, zero_b, kb)], axis=0)
            scores.append(_dot_nt(q.astype(BF16), k_bd) * dpair_ref[p])
            qs.append(q)
            ks.append(k)
            vs.append(v)
        fill(1)
        for p in range(n_pairs):
            q, k, v = qs[p], ks[p], vs[p]
            g = proj_s[blk(off_rg + p * LANES, c)]
            v_bd = jnp.concatenate([jnp.where(lo, v, zero_b), jnp.where(lo, zero_b, v)], axis=0)
            inner = _dot(scores[p].astype(BF16), v_bd)
            cross = _dot((q * qdec_ref[p]).astype(BF16), states[p].astype(BF16))
            kv = _dot_tn((k * kdec_ref[p]).astype(BF16), v)
            states[p] = states[p] * sdec_ref[p] + jnp.where(block_diag, kv, 0.0)
            o = inner + cross
            mu = _head_sum(o, lo) * (1.0 / HEAD_DIM)
            d = o - mu
            var = _head_sum(d * d, lo) * (1.0 / HEAD_DIM)
            o = d * lax.rsqrt(var + EPS) * gn_ref[0, :, pl.ds(p * LANES, LANES)]
            mix_s[blk(p * LANES, c)] = (o * _silu(g)).astype(BF16)

        def head_norm(a, gain):
            ms = _head_sum(a * a, lo) * (1.0 / HEAD_DIM)
            return a * lax.rsqrt(ms + EPS) * gain

        k = _rope(head_norm(proj_s[blk(off_ak, c)], kn_ref[0]), cos, sin, first_half)
        v = proj_s[blk(off_av, c)]
        k_sw = pltpu.roll(k, HEAD_DIM, 1)
        v_sw = pltpu.roll(v, HEAD_DIM, 1)
        k_cur = [jnp.where(lo, k, k_sw).astype(BF16), jnp.where(lo, k_sw, k).astype(BF16)]
        v_cur = [jnp.where(lo, v, v_sw).astype(BF16), jnp.where(lo, v_sw, v).astype(BF16)]
        qcols = [
            _rope(head_norm(proj_s[blk(off_aq + j * LANES, c)], qn_ref[0]), cos, sin, first_half)
            * (HEAD_DIM ** -0.5)
            for j in range(n_qcols)
        ]
        if c == 0:
            first_key = jnp.where(first_tile, CHUNK, 0)
            mask = band & (kj >= first_key)
        else:
            mask = band
        heads_out, swa_scores = [], []
        for grp in range(SWA_KV_HEADS):
            heads = [grp * group + j for j in range(group)]
            q_st = jnp.concatenate(
                [jnp.where(lo if hd % 2 == 0 else ~lo, qcols[hd // 2], 0.0).astype(BF16) for hd in heads],
                axis=0)
            k_win = jnp.concatenate([k_prev[grp], k_cur[grp]], axis=0)
            swa_scores.append(_dot_nt(q_st, k_win))
        fill(1)
        for grp in range(SWA_KV_HEADS):
            heads = [grp * group + j for j in range(group)]
            v_win = jnp.concatenate(
                [jnp.concatenate([v_prev[grp], v_cur[grp]], axis=0), ones_b], axis=1)
            sc = swa_scores[grp]
            probs, maxes = [], []
            for j, hd in enumerate(heads):
                s_h = jnp.where(mask, sc[j * CHUNK:(j + 1) * CHUNK], NEG_INF)
                m = jnp.maximum(jnp.max(s_h, axis=-1, keepdims=True), sinks_ref[layer, hd])
                probs.append(jnp.exp(s_h - m).astype(BF16))
                maxes.append(m)
            ov = _dot(jnp.concatenate(probs, axis=0), v_win)
            for j, hd in enumerate(heads):
                o_h = ov[j * CHUNK:(j + 1) * CHUNK]
                denom = o_h[:, LANES:] + jnp.exp(sinks_ref[layer, hd] - maxes[j])
                heads_out.append(o_h[:, :LANES] / denom)
        for j in range(n_qcols):
            mix_s[blk(ret_w + j * LANES, c)] = jnp.where(lo, heads_out[2 * j], heads_out[2 * j + 1]).astype(BF16)
        k_prev, v_prev = k_cur, v_cur

        z = proj_s[blk(off_cc, c, conv_c)] * proj_s[blk(off_cu, c, conv_c)]
        zbuf_s[pl.ds(CONV_HIST + c * CHUNK, CHUNK), :] = z
        y = convp_ref[0, CONV_K:CONV_K + 1, :]
        for back in range(CONV_K):
            z_back = z if back == 0 else zbuf_s[pl.ds(CONV_HIST - back + c * CHUNK, CHUNK), :]
            y = z_back * convp_ref[0, CONV_K - 1 - back:CONV_K - back, :] + y
        mix_s[blk(ret_w + swa_w, c, conv_c)] = (proj_s[blk(off_cb, c, conv_c)] * y).astype(BF16)

    for p in range(n_pairs):
        state_s[p] = states[p]
    for grp in range(SWA_KV_HEADS):
        kprev_s[grp] = k_prev[grp]
        vprev_s[grp] = v_prev[grp]
    zbuf_s[0:CONV_HIST, :] = zbuf_s[tile:tile + CONV_HIST, :]
    return x + gate1 * _dot(mix_s[...], wout_ref[0])


def _ffn_slab(j, h, wgu_ref, act_s):
    hidden = act_s.shape[1]
    cols = pl.ds(j * FFN_COL, FFN_COL)
    gt = _dot(h, wgu_ref[0, :, cols])
    up = _dot(h, wgu_ref[0, :, pl.ds(hidden + j * FFN_COL, FFN_COL)])
    act_s[:, cols] = (_silu(gt) * up).astype(BF16)


N_LAYER_INPUTS = 18


def _layer_kernel(dims, layer, n_cast, *refs):
    (sinks_ref, x_ref, mod_ref, gmix_ref, win_ref, cs_ref, convp_ref, qn_ref, kn_ref, gn_ref,
     dpair_ref, qdec_ref, kdec_ref, sdec_ref, wout_ref, gffn_ref, wgu_ref, wdown_ref) = refs[:N_LAYER_INPUTS]
    cast_in = refs[N_LAYER_INPUTS:N_LAYER_INPUTS + n_cast]
    o_ref = refs[N_LAYER_INPUTS + n_cast]
    cast_out = refs[N_LAYER_INPUTS + n_cast + 1:N_LAYER_INPUTS + 2 * n_cast + 1]
    (proj_s, mix_s, state_s, kprev_s, vprev_s, zbuf_s, xmid_s, xnew_s, act_s) = refs[N_LAYER_INPUTS + 2 * n_cast + 1:]
    cast_refs = tuple(zip(cast_in, cast_out))
    step = pl.program_id(0)

    @pl.when(step == 0)
    def _init():
        state_s[...] = jnp.zeros_like(state_s)
        kprev_s[...] = jnp.zeros_like(kprev_s)
        vprev_s[...] = jnp.zeros_like(vprev_s)
        zbuf_s[0:CONV_HIST, :] = jnp.zeros((CONV_HIST, zbuf_s.shape[1]), F32)
        xnew_s[...] = jnp.zeros_like(xnew_s)

    for src, dst in cast_refs:
        dst[...] = src[...].astype(BF16)

    xmid_s[...] = xnew_s[...]
    shift2, scale2, gate2 = mod_ref[0, 3], mod_ref[0, 4], mod_ref[0, 5]
    h2 = _modulated_rms_norm(xmid_s[...], gffn_ref[0] * (1.0 + scale2), shift2)
    slabs = iter(range(act_s.shape[1] // FFN_COL))

    def fill(n):
        for _ in range(n):
            j = next(slabs, None)
            if j is not None:
                _ffn_slab(j, h2, wgu_ref, act_s)

    xnew_s[...] = _mixer_tile(
        dims, layer, step == 0, x_ref[...], mod_ref, sinks_ref, gmix_ref, win_ref, cs_ref, convp_ref,
        qn_ref, kn_ref, gn_ref, dpair_ref, qdec_ref, kdec_ref, sdec_ref, wout_ref,
        proj_s, mix_s, state_s, kprev_s, vprev_s, zbuf_s, fill)
    fill(act_s.shape[1] // FFN_COL)
    o_ref[...] = xmid_s[...] + gate2 * _dot(act_s[...], wdown_ref[0])


def _fixed_block(block, index):
    return pl.BlockSpec(block, lambda i: index, pipeline_mode=pl.Buffered(1))


def _whole(arr):
    return _fixed_block(arr.shape, (0,) * arr.ndim)


def _layer_slice(arr, layer):
    return _fixed_block((1,) + arr.shape[1:], (layer,) + (0,) * (arr.ndim - 1))


def _cast_chunk_rows(rows, n_steps):
    return next(c for c in range(BF16_SUBLANES, rows + 1, BF16_SUBLANES)
                if rows % c == 0 and rows // c <= n_steps)


def _layer(layer, x, sinks, mod, g_mix, weights, cs_t, convp, qn, kn, gn, tables, g_ffn, next_f32, dims):
    seq, d = x.shape
    ret_w, swa_w, kv_w, conv_c = dims
    w_in, w_out, w_gu, w_down = weights
    in_w = w_in.shape[2]
    hidden = w_down.shape[1]
    n_pairs = ret_w // LANES
    dpair, qdec, kdec, sdec = tables
    tile = ROW_TILE
    n_tiles = seq // tile

    def mixer_rows(i):
        return (jnp.minimum(i, n_tiles - 1), 0)

    def ffn_rows(i):
        return (jnp.maximum(i - 1, 0), 0)

    def chunk_rows(i, which, last):
        return (which, jnp.minimum(i, last), 0)

    cast_in_specs, cast_out_specs, cast_out_shapes = [], [], []
    for w in next_f32:
        _, rows, cols = w.shape
        chunk = _cast_chunk_rows(rows, n_tiles)
        last = rows // chunk - 1
        cast_in_specs.append(
            pl.BlockSpec((1, chunk, cols), functools.partial(chunk_rows, which=layer + 1, last=last)))
        cast_out_specs.append(
            pl.BlockSpec((1, chunk, cols), functools.partial(chunk_rows, which=0, last=last)))
        cast_out_shapes.append(jax.ShapeDtypeStruct((1, rows, cols), BF16))

    outs = pl.pallas_call(
        functools.partial(_layer_kernel, dims, layer, len(next_f32)),
        grid=(n_tiles + 1,),
        in_specs=[
            pl.BlockSpec(memory_space=pltpu.SMEM),
            pl.BlockSpec((tile, d), mixer_rows),
            _layer_slice(mod, layer),
            _layer_slice(g_mix, layer),
            _whole(w_in),
            pl.BlockSpec((tile, HEAD_DIM), mixer_rows),
            _layer_slice(convp, layer),
            _layer_slice(qn, layer),
            _layer_slice(kn, layer),
            _layer_slice(gn, layer),
            _whole(dpair),
            _whole(qdec),
            _whole(kdec),
            _whole(sdec),
            _whole(w_out),
            _layer_slice(g_ffn, layer),
            _whole(w_gu),
            _whole(w_down),
        ] + cast_in_specs,
        out_specs=[pl.BlockSpec((tile, d), ffn_rows)] + cast_out_specs,
        out_shape=[jax.ShapeDtypeStruct((seq, d), F32)] + cast_out_shapes,
        scratch_shapes=[
            pltpu.VMEM((tile, in_w), F32),
            pltpu.VMEM((tile, d), BF16),
            pltpu.VMEM((n_pairs, LANES, LANES), F32),
            pltpu.VMEM((SWA_KV_HEADS, CHUNK, LANES), BF16),
            pltpu.VMEM((SWA_KV_HEADS, CHUNK, LANES), BF16),
            pltpu.VMEM((tile + CONV_HIST, conv_c), F32),
            pltpu.VMEM((tile, d), F32),
            pltpu.VMEM((tile, d), F32),
            pltpu.VMEM((tile, hidden), BF16),
        ],
        compiler_params=pltpu.CompilerParams(
            dimension_semantics=("arbitrary",), vmem_limit_bytes=VMEM_LIMIT_BYTES),
        name="layer",
    )(sinks, x, mod, g_mix, w_in, cs_t, convp, qn, kn, gn, dpair, qdec, kdec, sdec, w_out,
      g_ffn, w_gu, w_down, *next_f32)
    return outs[0], tuple(outs[1:])


def _rope_table(seq):
    half = HEAD_DIM // 2
    per_row = LANES // half
    inv = 1.0 / (ROPE_THETA ** (jnp.arange(0, HEAD_DIM, 2, dtype=F32) / HEAD_DIM))
    lane = jnp.arange(LANES)
    pos = (jnp.arange(seq // per_row)[:, None] * per_row + (lane // half)[None, :]).astype(F32)
    ang = pos * inv[lane % half][None, :]
    cos, sin = lax.optimization_barrier((jnp.cos(ang), jnp.sin(ang)))
    return jnp.concatenate([cos.reshape(seq, half), sin.reshape(seq, half)], axis=1)


def _retention_tables(ret_heads):
    log_gamma = jnp.log(1.0 - 2.0 ** (-5.0 - jnp.arange(ret_heads, dtype=F32)))
    idx = jnp.arange(CHUNK, dtype=F32)
    rel = idx[:, None] - idx[None, :]
    intra = jnp.where(rel >= 0, jnp.exp(log_gamma[:, None, None] * jnp.maximum(rel, 0.0)), 0.0)
    k_decay = jnp.exp(log_gamma[:, None] * (CHUNK - 1.0 - idx)[None, :])
    q_decay = jnp.exp(log_gamma[:, None] * (idx + 1.0)[None, :])
    chunk_decay = jnp.exp(log_gamma * CHUNK)
    n_pairs = ret_heads // 2

    def lanes(per_head):
        wide = jnp.repeat(per_head[:, :, None], HEAD_DIM, axis=2)
        return jnp.concatenate([wide[0::2], wide[1::2]], axis=2)

    dpair = jnp.concatenate([intra[0::2], intra[1::2]], axis=2)
    sdec = jnp.broadcast_to(
        jnp.repeat(chunk_decay.reshape(n_pairs, 2), HEAD_DIM, axis=1)[:, :, None], (n_pairs, LANES, LANES))
    return dpair, lanes(q_decay), lanes(k_decay), sdec


@jax.jit
def kernel(x, c, w_ada, b_ada, g_mix, w_in, conv_w, conv_b, q_norm, k_norm, sinks, ret_gn, w_out, g_ffn,
           w_gu, w_down):
    batch, seq, d = x.shape
    depth = w_ada.shape[0]
    assert batch == 1 and seq % ROW_TILE == 0
    ret_w = ret_gn.shape[1]
    swa_w = sinks.shape[1] * HEAD_DIM
    conv_c = conv_w.shape[2]
    kv_w = SWA_KV_HEADS * HEAD_DIM
    assert w_in.shape[2] == 4 * ret_w + swa_w + 2 * kv_w + 3 * conv_c
    assert ret_w % LANES == 0 and swa_w % LANES == 0 and kv_w == LANES and conv_c % LANES == 0
    assert w_down.shape[1] % FFN_COL == 0 and w_ada.shape[2] % ADA_COL == 0
    dims = (ret_w, swa_w, kv_w, conv_c)

    mod = _ada_modulation(c, w_ada, b_ada).reshape(depth, 6, 1, d)
    cs_t = _rope_table(seq)
    tables = _retention_tables(ret_w // HEAD_DIM)
    reps = LANES // HEAD_DIM
    convp = jnp.concatenate([conv_w, conv_b[:, None, :]], axis=1)
    qn = jnp.tile(q_norm, (1, reps))[:, None, :]
    kn = jnp.tile(k_norm, (1, reps))[:, None, :]

    f32_weights = (w_in, w_out, w_gu, w_down)
    weights = tuple(w[:1].astype(BF16) for w in f32_weights)
    xs = x[0]
    for l in range(depth):
        xs, weights = _layer(l, xs, sinks, mod, g_mix[:, None, :], weights, cs_t, convp, qn, kn,
                             ret_gn[:, None, :], tables, g_ffn[:, None, :],
                             f32_weights if l + 1 < depth else (), dims)
    return xs[None]
```

```python
import functools

import jax
import jax.numpy as jnp
from jax import lax
from jax.experimental import pallas as pl
from jax.experimental.pallas import tpu as pltpu

HEAD_DIM = 64
LANES = 128
BF16_SUBLANES = 16
SWA_KV_HEADS = 2
CONV_K = 3
CONV_HIST = 8
CHUNK = 128
ROPE_THETA = 10000.0
EPS = 1e-6
NEG_INF = -1e30
ROW_TILE = 512
FFN_COL = 256
FILL_BEFORE_PROJ = 3
ADA_COL = 2048
VMEM_LIMIT_BYTES = 60 * 1024 * 1024

ROW_SHIFT1, ROW_SCALE1, ROW_GATE1, ROW_SHIFT2, ROW_SCALE2, ROW_GATE2, ROW_GMIX, ROW_GFFN = range(8)
SMALL_GN, SMALL_QK, SMALL_CONV = 0, 1, 2

F32 = jnp.float32
BF16 = jnp.bfloat16


def _dot(a, b):
    return jnp.dot(a, b, preferred_element_type=F32)


def _dot_nt(a, b):
    return lax.dot_general(a, b, (((1,), (1,)), ((), ())), preferred_element_type=F32)


def _dot_tn(a, b):
    return lax.dot_general(a, b, (((0,), (0,)), ((), ())), preferred_element_type=F32)


def _lane_lo(shape):
    return lax.broadcasted_iota(jnp.int32, shape, 1) % LANES < HEAD_DIM


def _head_sum(a, lo):
    s_lo = jnp.sum(jnp.where(lo, a, 0.0), axis=-1, keepdims=True)
    s_hi = jnp.sum(jnp.where(lo, 0.0, a), axis=-1, keepdims=True)
    return jnp.where(lo, s_lo, s_hi)


def _rope(a, cos, sin_signed, first_half):
    width = a.shape[-1]
    fwd = pltpu.roll(a, width - HEAD_DIM // 2, 1)
    bwd = pltpu.roll(a, HEAD_DIM // 2, 1)
    return a * cos + jnp.where(first_half, fwd, bwd) * sin_signed


def _silu(a):
    return a * jax.nn.sigmoid(a)


def _modulated_rms_norm(x, gain_scale, shift):
    ms = jnp.mean(x * x, axis=-1, keepdims=True)
    return (x * lax.rsqrt(ms + EPS) * gain_scale + shift).astype(BF16)


def _ada_kernel(c_ref, w_ref, b_ref, o_ref):
    s = _silu(c_ref[...])
    for j in range(o_ref.shape[2] // LANES):
        cols = pl.ds(j * LANES, LANES)
        o_ref[0, :, cols] = jnp.sum(w_ref[0, :, cols] * s, axis=0, keepdims=True) + b_ref[0, :, cols]


def _ada_modulation(c, w_ada, b_ada):
    depth, d, e = w_ada.shape
    c_lanes = jnp.broadcast_to(c.reshape(d, 1), (d, LANES))
    return pl.pallas_call(
        _ada_kernel,
        grid=(depth, e // ADA_COL),
        in_specs=[
            pl.BlockSpec((d, LANES), lambda l, j: (0, 0)),
            pl.BlockSpec((1, d, ADA_COL), lambda l, j: (l, 0, j)),
            pl.BlockSpec((1, 1, ADA_COL), lambda l, j: (l, 0, j)),
        ],
        out_specs=pl.BlockSpec((1, 1, ADA_COL), lambda l, j: (l, 0, j)),
        out_shape=jax.ShapeDtypeStruct((depth, 1, e), F32),
        compiler_params=pltpu.CompilerParams(
            dimension_semantics=("arbitrary", "arbitrary"), vmem_limit_bytes=VMEM_LIMIT_BYTES),
        name="ada_modulation",
    )(c_lanes, w_ada, b_ada.reshape(depth, 1, e))


def _mixer_tile(dims, layer, first_tile, x, sinks_ref, rows_ref, small_ref, tables_ref, win_ref, cs_ref, wout_ref,
                proj_s, mix_s, state_s, kprev_s, vprev_s, zbuf_s, fill):
    ret_w, swa_w, kv_w, conv_c = dims
    n_pairs = ret_w // LANES
    n_qcols = swa_w // LANES
    group = (swa_w // HEAD_DIM) // SWA_KV_HEADS
    tile = x.shape[0]

    off_rq, off_rk, off_rv, off_rg = 0, ret_w, 2 * ret_w, 3 * ret_w
    off_aq = 4 * ret_w
    off_ak = off_aq + swa_w
    off_av = off_ak + kv_w
    off_cb = off_av + kv_w
    off_cc = off_cb + conv_c
    off_cu = off_cc + conv_c

    def param_row(r):
        return rows_ref[0, r:r + 1, :]

    h = _modulated_rms_norm(x, param_row(ROW_GMIX) * (1.0 + param_row(ROW_SCALE1)), param_row(ROW_SHIFT1))
    fill(FILL_BEFORE_PROJ)
    proj_s[...] = _dot(h, win_ref[0])

    sq = (CHUNK, LANES)
    lo = _lane_lo(sq)
    first_half = lax.broadcasted_iota(jnp.int32, sq, 1) % HEAD_DIM < HEAD_DIM // 2
    row = lax.broadcasted_iota(jnp.int32, sq, 0)
    col = lax.broadcasted_iota(jnp.int32, sq, 1)
    block_diag = (row < HEAD_DIM) == (col < HEAD_DIM)
    qi = lax.broadcasted_iota(jnp.int32, (CHUNK, 2 * CHUNK), 0)
    kj = lax.broadcasted_iota(jnp.int32, (CHUNK, 2 * CHUNK), 1)
    band = (kj > qi) & (kj <= qi + CHUNK)
    zero_b = jnp.zeros(sq, BF16)
    reps = LANES // (HEAD_DIM // 2)
    sin_sign = jnp.where(first_half, -1.0, 1.0)
    ones_b = jnp.ones((2 * CHUNK, LANES), BF16)

    def blk(off, c, width=LANES):
        return (pl.ds(c * CHUNK, CHUNK), pl.ds(off, width))

    def table(p, off, width=LANES):
        return tables_ref[p, :, pl.ds(off, width)]

    q_gain = small_ref[0, SMALL_QK:SMALL_QK + 1, 0:LANES]
    k_gain = small_ref[0, SMALL_QK:SMALL_QK + 1, LANES:2 * LANES]

    states = [state_s[p] for p in range(n_pairs)]
    k_prev = [kprev_s[grp] for grp in range(SWA_KV_HEADS)]
    v_prev = [vprev_s[grp] for grp in range(SWA_KV_HEADS)]

    for c in range(tile // CHUNK):
        rows = pl.ds(c * CHUNK, CHUNK)
        cs = cs_ref[rows, :]
        cos = jnp.concatenate([cs[:, :HEAD_DIM // 2]] * reps, axis=1)
        sin = jnp.concatenate([cs[:, HEAD_DIM // 2:]] * reps, axis=1) * sin_sign

        qs, ks, vs, scores = [], [], [], []
        for p in range(n_pairs):
            q = _rope(proj_s[blk(off_rq + p * LANES, c)], cos, sin, first_half)
            k = _rope(proj_s[blk(off_rk + p * LANES, c)], cos, sin, first_half) * (HEAD_DIM ** -0.5)
            v = proj_s[blk(off_rv + p * LANES, c)].astype(BF16)
            kb = k.astype(BF16)
            k_bd = jnp.concatenate([jnp.where(lo, kb, zero_b), jnp.where(lo, zero_b, kb)], axis=0)
            scores.append(_dot_nt(q.astype(BF16), k_bd) * table(p, 0, 2 * CHUNK))
            qs.append(q)
            ks.append(k)
            vs.append(v)
        fill(1)
        for p in range(n_pairs):
            q, k, v = qs[p], ks[p], vs[p]
            g = proj_s[blk(off_rg + p * LANES, c)]
            v_bd = jnp.concatenate([jnp.where(lo, v, zero_b), jnp.where(lo, zero_b, v)], axis=0)
            inner = _dot(scores[p].astype(BF16), v_bd)
            cross = _dot((q * table(p, 2 * CHUNK)).astype(BF16), states[p].astype(BF16))
            kv = _dot_tn((k * table(p, 2 * CHUNK + LANES)).astype(BF16), v)
            states[p] = states[p] * table(p, 2 * CHUNK + 2 * LANES) + jnp.where(block_diag, kv, 0.0)
            o = inner + cross
            mu = _head_sum(o, lo) * (1.0 / HEAD_DIM)
            d = o - mu
            var = _head_sum(d * d, lo) * (1.0 / HEAD_DIM)
            o = d * lax.rsqrt(var + EPS) * small_ref[0, SMALL_GN:SMALL_GN + 1, pl.ds(p * LANES, LANES)]
            mix_s[blk(p * LANES, c)] = (o * _silu(g)).astype(BF16)

        def head_norm(a, gain):
            ms = _head_sum(a * a, lo) * (1.0 / HEAD_DIM)
            return a * lax.rsqrt(ms + EPS) * gain

        k = _rope(head_norm(proj_s[blk(off_ak, c)], k_gain), cos, sin, first_half)
        v = proj_s[blk(off_av, c)]
        k_sw = pltpu.roll(k, HEAD_DIM, 1)
        v_sw = pltpu.roll(v, HEAD_DIM, 1)
        k_cur = [jnp.where(lo, k, k_sw).astype(BF16), jnp.where(lo, k_sw, k).astype(BF16)]
        v_cur = [jnp.where(lo, v, v_sw).astype(BF16), jnp.where(lo, v_sw, v).astype(BF16)]
        qcols = [
            _rope(head_norm(proj_s[blk(off_aq + j * LANES, c)], q_gain), cos, sin, first_half)
            * (HEAD_DIM ** -0.5)
            for j in range(n_qcols)
        ]
        if c == 0:
            first_key = jnp.where(first_tile, CHUNK, 0)
            mask = band & (kj >= first_key)
        else:
            mask = band
        heads_out, swa_scores = [], []
        for grp in range(SWA_KV_HEADS):
            heads = [grp * group + j for j in range(group)]
            q_st = jnp.concatenate(
                [jnp.where(lo if hd % 2 == 0 else ~lo, qcols[hd // 2], 0.0).astype(BF16) for hd in heads],
                axis=0)
            k_win = jnp.concatenate([k_prev[grp], k_cur[grp]], axis=0)
            swa_scores.append(_dot_nt(q_st, k_win))
        fill(1)
        for grp in range(SWA_KV_HEADS):
            heads = [grp * group + j for j in range(group)]
            v_win = jnp.concatenate(
                [jnp.concatenate([v_prev[grp], v_cur[grp]], axis=0), ones_b], axis=1)
            sc = swa_scores[grp]
            probs, maxes = [], []
            for j, hd in enumerate(heads):
                s_h = jnp.where(mask, sc[j * CHUNK:(j + 1) * CHUNK], NEG_INF)
                m = jnp.maximum(jnp.max(s_h, axis=-1, keepdims=True), sinks_ref[layer, hd])
                probs.append(jnp.exp(s_h - m).astype(BF16))
                maxes.append(m)
            ov = _dot(jnp.concatenate(probs, axis=0), v_win)
            for j, hd in enumerate(heads):
                o_h = ov[j * CHUNK:(j + 1) * CHUNK]
                denom = o_h[:, LANES:] + jnp.exp(sinks_ref[layer, hd] - maxes[j])
                heads_out.append(o_h[:, :LANES] / denom)
        for j in range(n_qcols):
            mix_s[blk(ret_w + j * LANES, c)] = jnp.where(lo, heads_out[2 * j], heads_out[2 * j + 1]).astype(BF16)
        k_prev, v_prev = k_cur, v_cur

        z = proj_s[blk(off_cc, c, conv_c)] * proj_s[blk(off_cu, c, conv_c)]
        zbuf_s[pl.ds(CONV_HIST + c * CHUNK, CHUNK), :] = z
        def conv_row(r):
            return small_ref[0, SMALL_CONV + r:SMALL_CONV + r + 1, 0:conv_c]

        y = conv_row(CONV_K)
        for back in range(CONV_K):
            z_back = z if back == 0 else zbuf_s[pl.ds(CONV_HIST - back + c * CHUNK, CHUNK), :]
            y = z_back * conv_row(CONV_K - 1 - back) + y
        mix_s[blk(ret_w + swa_w, c, conv_c)] = (proj_s[blk(off_cb, c, conv_c)] * y).astype(BF16)

    for p in range(n_pairs):
        state_s[p] = states[p]
    for grp in range(SWA_KV_HEADS):
        kprev_s[grp] = k_prev[grp]
        vprev_s[grp] = v_prev[grp]
    zbuf_s[0:CONV_HIST, :] = zbuf_s[tile:tile + CONV_HIST, :]
    return x + param_row(ROW_GATE1) * _dot(mix_s[...], wout_ref[0])


def _ffn_slab(j, h, wgu_ref, act_s):
    hidden = act_s.shape[1]
    cols = pl.ds(j * FFN_COL, FFN_COL)
    gt = _dot(h, wgu_ref[0, :, cols])
    up = _dot(h, wgu_ref[0, :, pl.ds(hidden + j * FFN_COL, FFN_COL)])
    act_s[:, cols] = (_silu(gt) * up).astype(BF16)


N_LAYER_INPUTS = 10


def _layer_kernel(dims, layer, n_cast, *refs):
    (sinks_ref, x_ref, rows_ref, small_ref, tables_ref, win_ref, cs_ref, wout_ref, wgu_ref, wdown_ref) = (
        refs[:N_LAYER_INPUTS])
    cast_in = refs[N_LAYER_INPUTS:N_LAYER_INPUTS + n_cast]
    o_ref = refs[N_LAYER_INPUTS + n_cast]
    cast_out = refs[N_LAYER_INPUTS + n_cast + 1:N_LAYER_INPUTS + 2 * n_cast + 1]
    (proj_s, mix_s, state_s, kprev_s, vprev_s, zbuf_s, xmid_s, xnew_s, act_s) = refs[N_LAYER_INPUTS + 2 * n_cast + 1:]
    cast_refs = tuple(zip(cast_in, cast_out))
    step = pl.program_id(0)

    @pl.when(step == 0)
    def _init():
        state_s[...] = jnp.zeros_like(state_s)
        kprev_s[...] = jnp.zeros_like(kprev_s)
        vprev_s[...] = jnp.zeros_like(vprev_s)
        zbuf_s[0:CONV_HIST, :] = jnp.zeros((CONV_HIST, zbuf_s.shape[1]), F32)
        xnew_s[...] = jnp.zeros_like(xnew_s)

    for src, dst in cast_refs:
        dst[...] = src[...].astype(BF16)

    xmid_s[...] = xnew_s[...]
    def param_row(r):
        return rows_ref[0, r:r + 1, :]

    h2 = _modulated_rms_norm(xmid_s[...], param_row(ROW_GFFN) * (1.0 + param_row(ROW_SCALE2)), param_row(ROW_SHIFT2))
    slabs = iter(range(act_s.shape[1] // FFN_COL))

    def fill(n):
        for _ in range(n):
            j = next(slabs, None)
            if j is not None:
                _ffn_slab(j, h2, wgu_ref, act_s)

    xnew_s[...] = _mixer_tile(
        dims, layer, step == 0, x_ref[...], sinks_ref, rows_ref, small_ref, tables_ref, win_ref, cs_ref, wout_ref,
        proj_s, mix_s, state_s, kprev_s, vprev_s, zbuf_s, fill)
    fill(act_s.shape[1] // FFN_COL)
    o_ref[...] = xmid_s[...] + param_row(ROW_GATE2) * _dot(act_s[...], wdown_ref[0])


def _fixed_block(block, index):
    return pl.BlockSpec(block, lambda i: index, pipeline_mode=pl.Buffered(1))


def _whole(arr):
    return _fixed_block(arr.shape, (0,) * arr.ndim)


def _layer_slice(arr, layer):
    return _fixed_block((1,) + arr.shape[1:], (layer,) + (0,) * (arr.ndim - 1))


def _cast_chunk_rows(rows, n_steps):
    return next(c for c in range(BF16_SUBLANES, rows + 1, BF16_SUBLANES)
                if rows % c == 0 and rows // c <= n_steps)


def _layer(layer, x, sinks, row_params, small, tables, weights, cs_t, next_f32, dims):
    seq, d = x.shape
    ret_w, swa_w, kv_w, conv_c = dims
    w_in, w_out, w_gu, w_down = weights
    in_w = w_in.shape[2]
    hidden = w_down.shape[1]
    n_pairs = ret_w // LANES
    tile = ROW_TILE
    n_tiles = seq // tile

    def mixer_rows(i):
        return (jnp.minimum(i, n_tiles - 1), 0)

    def ffn_rows(i):
        return (jnp.maximum(i - 1, 0), 0)

    def chunk_rows(i, which, last):
        return (which, jnp.minimum(i, last), 0)

    cast_in_specs, cast_out_specs, cast_out_shapes = [], [], []
    for w in next_f32:
        _, rows, cols = w.shape
        chunk = _cast_chunk_rows(rows, n_tiles)
        last = rows // chunk - 1
        cast_in_specs.append(
            pl.BlockSpec((1, chunk, cols), functools.partial(chunk_rows, which=layer + 1, last=last)))
        cast_out_specs.append(
            pl.BlockSpec((1, chunk, cols), functools.partial(chunk_rows, which=0, last=last)))
        cast_out_shapes.append(jax.ShapeDtypeStruct((1, rows, cols), BF16))

    outs = pl.pallas_call(
        functools.partial(_layer_kernel, dims, layer, len(next_f32)),
        grid=(n_tiles + 1,),
        in_specs=[
            pl.BlockSpec(memory_space=pltpu.SMEM),
            pl.BlockSpec((tile, d), mixer_rows),
            _layer_slice(row_params, layer),
            _layer_slice(small, layer),
            _whole(tables),
            _whole(w_in),
            pl.BlockSpec((tile, HEAD_DIM), mixer_rows),
            _whole(w_out),
            _whole(w_gu),
            _whole(w_down),
        ] + cast_in_specs,
        out_specs=[pl.BlockSpec((tile, d), ffn_rows)] + cast_out_specs,
        out_shape=[jax.ShapeDtypeStruct((seq, d), F32)] + cast_out_shapes,
        scratch_shapes=[
            pltpu.VMEM((tile, in_w), F32),
            pltpu.VMEM((tile, d), BF16),
            pltpu.VMEM((n_pairs, LANES, LANES), F32),
            pltpu.VMEM((SWA_KV_HEADS, CHUNK, LANES), BF16),
            pltpu.VMEM((SWA_KV_HEADS, CHUNK, LANES), BF16),
            pltpu.VMEM((tile + CONV_HIST, conv_c), F32),
            pltpu.VMEM((tile, d), F32),
            pltpu.VMEM((tile, d), F32),
            pltpu.VMEM((tile, hidden), BF16),
        ],
        compiler_params=pltpu.CompilerParams(
            dimension_semantics=("arbitrary",), vmem_limit_bytes=VMEM_LIMIT_BYTES),
        name="layer",
    )(sinks, x, row_params, small, tables, w_in, cs_t, w_out, w_gu, w_down, *next_f32)
    return outs[0], tuple(outs[1:])


def _rope_table(seq):
    half = HEAD_DIM // 2
    per_row = LANES // half
    inv = 1.0 / (ROPE_THETA ** (jnp.arange(0, HEAD_DIM, 2, dtype=F32) / HEAD_DIM))
    lane = jnp.arange(LANES)
    pos = (jnp.arange(seq // per_row)[:, None] * per_row + (lane // half)[None, :]).astype(F32)
    ang = pos * inv[lane % half][None, :]
    cos, sin = lax.optimization_barrier((jnp.cos(ang), jnp.sin(ang)))
    return jnp.concatenate([cos.reshape(seq, half), sin.reshape(seq, half)], axis=1)


def _retention_tables(ret_heads):
    log_gamma = jnp.log(1.0 - 2.0 ** (-5.0 - jnp.arange(ret_heads, dtype=F32)))
    idx = jnp.arange(CHUNK, dtype=F32)
    rel = idx[:, None] - idx[None, :]
    intra = jnp.where(rel >= 0, jnp.exp(log_gamma[:, None, None] * jnp.maximum(rel, 0.0)), 0.0)
    k_decay = jnp.exp(log_gamma[:, None] * (CHUNK - 1.0 - idx)[None, :])
    q_decay = jnp.exp(log_gamma[:, None] * (idx + 1.0)[None, :])
    chunk_decay = jnp.exp(log_gamma * CHUNK)
    n_pairs = ret_heads // 2

    def lanes(per_head):
        wide = jnp.repeat(per_head[:, :, None], HEAD_DIM, axis=2)
        return jnp.concatenate([wide[0::2], wide[1::2]], axis=2)

    dpair = jnp.concatenate([intra[0::2], intra[1::2]], axis=2)
    sdec = jnp.broadcast_to(
        jnp.repeat(chunk_decay.reshape(n_pairs, 2), HEAD_DIM, axis=1)[:, :, None], (n_pairs, LANES, LANES))
    return jnp.concatenate([dpair, lanes(q_decay), lanes(k_decay), sdec], axis=2)


@jax.jit
def kernel(x, c, w_ada, b_ada, g_mix, w_in, conv_w, conv_b, q_norm, k_norm, sinks, ret_gn, w_out, g_ffn,
           w_gu, w_down):
    batch, seq, d = x.shape
    depth = w_ada.shape[0]
    assert batch == 1 and seq % ROW_TILE == 0
    ret_w = ret_gn.shape[1]
    swa_w = sinks.shape[1] * HEAD_DIM
    conv_c = conv_w.shape[2]
    kv_w = SWA_KV_HEADS * HEAD_DIM
    assert w_in.shape[2] == 4 * ret_w + swa_w + 2 * kv_w + 3 * conv_c
    assert ret_w % LANES == 0 and swa_w % LANES == 0 and kv_w == LANES and conv_c % LANES == 0
    assert w_down.shape[1] % FFN_COL == 0 and w_ada.shape[2] % ADA_COL == 0
    dims = (ret_w, swa_w, kv_w, conv_c)

    mod = _ada_modulation(c, w_ada, b_ada).reshape(depth, 6, d)
    cs_t = _rope_table(seq)
    tables = _retention_tables(ret_w // HEAD_DIM)
    reps = LANES // HEAD_DIM
    row_params = jnp.concatenate([mod, g_mix[:, None, :], g_ffn[:, None, :]], axis=1)
    small_w = max(ret_w, 2 * LANES, conv_c)

    def widen(a):
        return jnp.pad(a, ((0, 0), (0, 0), (0, small_w - a.shape[2])))

    qk = jnp.concatenate([jnp.tile(q_norm, (1, reps)), jnp.tile(k_norm, (1, reps))], axis=1)[:, None, :]
    small = jnp.concatenate([widen(ret_gn[:, None, :]), widen(qk), widen(conv_w), widen(conv_b[:, None, :])], axis=1)
    small = jnp.pad(small, ((0, 0), (0, 8 - small.shape[1]), (0, 0)))

    f32_weights = (w_in, w_out, w_gu, w_down)
    weights = tuple(w[:1].astype(BF16) for w in f32_weights)
    xs = x[0]
    for l in range(depth):
        xs, weights = _layer(l, xs, sinks, row_params, small, tables, weights, cs_t,
                             f32_weights if l + 1 < depth else (), dims)
    return xs[None]
```

```python
import functools

import jax
import jax.numpy as jnp
from jax import lax
from jax.experimental import pallas as pl
from jax.experimental.pallas import tpu as pltpu

HEAD_DIM = 64
LANES = 128
BF16_SUBLANES = 16
SWA_KV_HEADS = 2
CONV_K = 3
CONV_HIST = 8
CHUNK = 128
ROPE_THETA = 10000.0
EPS = 1e-6
NEG_INF = -1e30
ROW_TILE = 512
FFN_COL = 256
FILL_BEFORE_PROJ = 3
ADA_COL = 2048
VMEM_LIMIT_BYTES = 60 * 1024 * 1024

ROW_SHIFT1, ROW_SCALE1, ROW_GATE1, ROW_SHIFT2, ROW_SCALE2, ROW_GATE2, ROW_GMIX, ROW_GFFN = range(8)
SMALL_GN, SMALL_QK, SMALL_CONV = 0, 1, 2

F32 = jnp.float32
BF16 = jnp.bfloat16


def _dot(a, b):
    return jnp.dot(a, b, preferred_element_type=F32)


def _dot_nt(a, b):
    return lax.dot_general(a, b, (((1,), (1,)), ((), ())), preferred_element_type=F32)


def _dot_tn(a, b):
    return lax.dot_general(a, b, (((0,), (0,)), ((), ())), preferred_element_type=F32)


def _lane_lo(shape):
    return lax.broadcasted_iota(jnp.int32, shape, 1) % LANES < HEAD_DIM


def _head_sum(a, lo):
    s_lo = jnp.sum(jnp.where(lo, a, 0.0), axis=-1, keepdims=True)
    s_hi = jnp.sum(jnp.where(lo, 0.0, a), axis=-1, keepdims=True)
    return jnp.where(lo, s_lo, s_hi)


def _rope(a, cos, sin_signed, first_half):
    width = a.shape[-1]
    fwd = pltpu.roll(a, width - HEAD_DIM // 2, 1)
    bwd = pltpu.roll(a, HEAD_DIM // 2, 1)
    return a * cos + jnp.where(first_half, fwd, bwd) * sin_signed


def _silu(a):
    return a * jax.nn.sigmoid(a)


def _modulated_rms_norm(x, gain_scale, shift):
    ms = jnp.mean(x * x, axis=-1, keepdims=True)
    return (x * lax.rsqrt(ms + EPS) * gain_scale + shift).astype(BF16)


def _ada_kernel(c_ref, w_ref, b_ref, o_ref):
    s = _silu(c_ref[...])
    for j in range(o_ref.shape[2] // LANES):
        cols = pl.ds(j * LANES, LANES)
        o_ref[0, :, cols] = jnp.sum(w_ref[0, :, cols] * s, axis=0, keepdims=True) + b_ref[0, :, cols]


def _ada_modulation(c, w_ada, b_ada):
    depth, d, e = w_ada.shape
    c_lanes = jnp.broadcast_to(c.reshape(d, 1), (d, LANES))
    return pl.pallas_call(
        _ada_kernel,
        grid=(depth, e // ADA_COL),
        in_specs=[
            pl.BlockSpec((d, LANES), lambda l, j: (0, 0)),
            pl.BlockSpec((1, d, ADA_COL), lambda l, j: (l, 0, j)),
            pl.BlockSpec((1, 1, ADA_COL), lambda l, j: (l, 0, j)),
        ],
        out_specs=pl.BlockSpec((1, 1, ADA_COL), lambda l, j: (l, 0, j)),
        out_shape=jax.ShapeDtypeStruct((depth, 1, e), F32),
        compiler_params=pltpu.CompilerParams(
            dimension_semantics=("arbitrary", "arbitrary"), vmem_limit_bytes=VMEM_LIMIT_BYTES),
        name="ada_modulation",
    )(c_lanes, w_ada, b_ada.reshape(depth, 1, e))


def _mixer_tile(dims, layer, first_tile, x, sinks_ref, rows_ref, small_ref, tables_ref, win_ref, cos_ref, sin_ref,
                wout_ref, proj_s, mix_s, state_s, kprev_s, vprev_s, zbuf_s, fill):
    ret_w, swa_w, kv_w, conv_c = dims
    n_pairs = ret_w // LANES
    n_qcols = swa_w // LANES
    group = (swa_w // HEAD_DIM) // SWA_KV_HEADS
    tile = x.shape[0]

    off_rq, off_rk, off_rv, off_rg = 0, ret_w, 2 * ret_w, 3 * ret_w
    off_aq = 4 * ret_w
    off_ak = off_aq + swa_w
    off_av = off_ak + kv_w
    off_cb = off_av + kv_w
    off_cc = off_cb + conv_c
    off_cu = off_cc + conv_c

    def param_row(r):
        return rows_ref[0, r:r + 1, :]

    h = _modulated_rms_norm(x, param_row(ROW_GMIX) * (1.0 + param_row(ROW_SCALE1)), param_row(ROW_SHIFT1))
    fill(FILL_BEFORE_PROJ)
    proj_s[...] = _dot(h, win_ref[0])

    sq = (CHUNK, LANES)
    lo = _lane_lo(sq)
    first_half = lax.broadcasted_iota(jnp.int32, sq, 1) % HEAD_DIM < HEAD_DIM // 2
    row = lax.broadcasted_iota(jnp.int32, sq, 0)
    col = lax.broadcasted_iota(jnp.int32, sq, 1)
    block_diag = (row < HEAD_DIM) == (col < HEAD_DIM)
    qi = lax.broadcasted_iota(jnp.int32, (CHUNK, 2 * CHUNK), 0)
    kj = lax.broadcasted_iota(jnp.int32, (CHUNK, 2 * CHUNK), 1)
    band = (kj > qi) & (kj <= qi + CHUNK)
    zero_b = jnp.zeros(sq, BF16)
    reps = LANES // (HEAD_DIM // 2)
    sin_sign = jnp.where(first_half, -1.0, 1.0)
    ones_b = jnp.ones((2 * CHUNK, LANES), BF16)

    def blk(off, c, width=LANES):
        return (pl.ds(c * CHUNK, CHUNK), pl.ds(off, width))

    def table(p, off, width=LANES):
        return tables_ref[p, :, pl.ds(off, width)]

    q_gain = small_ref[0, SMALL_QK:SMALL_QK + 1, 0:LANES]
    k_gain = small_ref[0, SMALL_QK:SMALL_QK + 1, LANES:2 * LANES]

    states = [state_s[p] for p in range(n_pairs)]
    k_prev = [kprev_s[grp] for grp in range(SWA_KV_HEADS)]
    v_prev = [vprev_s[grp] for grp in range(SWA_KV_HEADS)]

    for c in range(tile // CHUNK):
        rows = pl.ds(c * CHUNK, CHUNK)
        cos = jnp.concatenate([cos_ref[rows, :]] * reps, axis=1)
        sin = jnp.concatenate([sin_ref[rows, :]] * reps, axis=1) * sin_sign

        qs, ks, vs, scores = [], [], [], []
        for p in range(n_pairs):
            q = _rope(proj_s[blk(off_rq + p * LANES, c)], cos, sin, first_half)
            k = _rope(proj_s[blk(off_rk + p * LANES, c)], cos, sin, first_half) * (HEAD_DIM ** -0.5)
            v = proj_s[blk(off_rv + p * LANES, c)].astype(BF16)
            kb = k.astype(BF16)
            k_bd = jnp.concatenate([jnp.where(lo, kb, zero_b), jnp.where(lo, zero_b, kb)], axis=0)
            scores.append(_dot_nt(q.astype(BF16), k_bd) * table(p, 0, 2 * CHUNK))
            qs.append(q)
            ks.append(k)
            vs.append(v)
        fill(1)
        for p in range(n_pairs):
            q, k, v = qs[p], ks[p], vs[p]
            g = proj_s[blk(off_rg + p * LANES, c)]
            v_bd = jnp.concatenate([jnp.where(lo, v, zero_b), jnp.where(lo, zero_b, v)], axis=0)
            inner = _dot(scores[p].astype(BF16), v_bd)
            cross = _dot((q * table(p, 2 * CHUNK)).astype(BF16), states[p].astype(BF16))
            kv = _dot_tn((k * table(p, 2 * CHUNK + LANES)).astype(BF16), v)
            states[p] = states[p] * table(p, 2 * CHUNK + 2 * LANES) + jnp.where(block_diag, kv, 0.0)
            o = inner + cross
            mu = _head_sum(o, lo) * (1.0 / HEAD_DIM)
            d = o - mu
            var = _head_sum(d * d, lo) * (1.0 / HEAD_DIM)
            o = d * lax.rsqrt(var + EPS) * small_ref[0, SMALL_GN:SMALL_GN + 1, pl.ds(p * LANES, LANES)]
            mix_s[blk(p * LANES, c)] = (o * _silu(g)).astype(BF16)

        def head_norm(a, gain):
            ms = _head_sum(a * a, lo) * (1.0 / HEAD_DIM)
            return a * lax.rsqrt(ms + EPS) * gain

        k = _rope(head_norm(proj_s[blk(off_ak, c)], k_gain), cos, sin, first_half)
        v = proj_s[blk(off_av, c)]
        k_sw = pltpu.roll(k, HEAD_DIM, 1)
        v_sw = pltpu.roll(v, HEAD_DIM, 1)
        k_cur = [jnp.where(lo, k, k_sw).astype(BF16), jnp.where(lo, k_sw, k).astype(BF16)]
        v_cur = [jnp.where(lo, v, v_sw).astype(BF16), jnp.where(lo, v_sw, v).astype(BF16)]
        qcols = [
            _rope(head_norm(proj_s[blk(off_aq + j * LANES, c)], q_gain), cos, sin, first_half)
            * (HEAD_DIM ** -0.5)
            for j in range(n_qcols)
        ]
        if c == 0:
            first_key = jnp.where(first_tile, CHUNK, 0)
            mask = band & (kj >= first_key)
        else:
            mask = band
        heads_out, swa_scores = [], []
        for grp in range(SWA_KV_HEADS):
            heads = [grp * group + j for j in range(group)]
            q_st = jnp.concatenate(
                [jnp.where(lo if hd % 2 == 0 else ~lo, qcols[hd // 2], 0.0).astype(BF16) for hd in heads],
                axis=0)
            k_win = jnp.concatenate([k_prev[grp], k_cur[grp]], axis=0)
            swa_scores.append(_dot_nt(q_st, k_win))
        fill(1)
        for grp in range(SWA_KV_HEADS):
            heads = [grp * group + j for j in range(group)]
            v_win = jnp.concatenate(
                [jnp.concatenate([v_prev[grp], v_cur[grp]], axis=0), ones_b], axis=1)
            sc = swa_scores[grp]
            probs, maxes = [], []
            for j, hd in enumerate(heads):
                s_h = jnp.where(mask, sc[j * CHUNK:(j + 1) * CHUNK], NEG_INF)
                m = jnp.maximum(jnp.max(s_h, axis=-1, keepdims=True), sinks_ref[layer, hd])
                probs.append(jnp.exp(s_h - m).astype(BF16))
                maxes.append(m)
            ov = _dot(jnp.concatenate(probs, axis=0), v_win)
            for j, hd in enumerate(heads):
                o_h = ov[j * CHUNK:(j + 1) * CHUNK]
                denom = o_h[:, LANES:] + jnp.exp(sinks_ref[layer, hd] - maxes[j])
                heads_out.append(o_h[:, :LANES] / denom)
        for j in range(n_qcols):
            mix_s[blk(ret_w + j * LANES, c)] = jnp.where(lo, heads_out[2 * j], heads_out[2 * j + 1]).astype(BF16)
        k_prev, v_prev = k_cur, v_cur

        z = proj_s[blk(off_cc, c, conv_c)] * proj_s[blk(off_cu, c, conv_c)]
        zbuf_s[pl.ds(CONV_HIST + c * CHUNK, CHUNK), :] = z
        def conv_row(r):
            return small_ref[0, SMALL_CONV + r:SMALL_CONV + r + 1, 0:conv_c]

        y = conv_row(CONV_K)
        for back in range(CONV_K):
            z_back = z if back == 0 else zbuf_s[pl.ds(CONV_HIST - back + c * CHUNK, CHUNK), :]
            y = z_back * conv_row(CONV_K - 1 - back) + y
        mix_s[blk(ret_w + swa_w, c, conv_c)] = (proj_s[blk(off_cb, c, conv_c)] * y).astype(BF16)

    for p in range(n_pairs):
        state_s[p] = states[p]
    for grp in range(SWA_KV_HEADS):
        kprev_s[grp] = k_prev[grp]
        vprev_s[grp] = v_prev[grp]
    zbuf_s[0:CONV_HIST, :] = zbuf_s[tile:tile + CONV_HIST, :]
    return x + param_row(ROW_GATE1) * _dot(mix_s[...], wout_ref[0])


def _ffn_slab(j, h, wgu_ref, act_s):
    hidden = act_s.shape[1]
    cols = pl.ds(j * FFN_COL, FFN_COL)
    gt = _dot(h, wgu_ref[0, :, cols])
    up = _dot(h, wgu_ref[0, :, pl.ds(hidden + j * FFN_COL, FFN_COL)])
    act_s[:, cols] = (_silu(gt) * up).astype(BF16)


N_LAYER_INPUTS = 11


def _layer_kernel(dims, layer, n_cast, *refs):
    (sinks_ref, x_ref, rows_ref, small_ref, tables_ref, win_ref, cos_ref, sin_ref, wout_ref, wgu_ref,
     wdown_ref) = refs[:N_LAYER_INPUTS]
    cast_in = refs[N_LAYER_INPUTS:N_LAYER_INPUTS + n_cast]
    o_ref = refs[N_LAYER_INPUTS + n_cast]
    cast_out = refs[N_LAYER_INPUTS + n_cast + 1:N_LAYER_INPUTS + 2 * n_cast + 1]
    (proj_s, mix_s, state_s, kprev_s, vprev_s, zbuf_s, xmid_s, xnew_s, act_s) = refs[N_LAYER_INPUTS + 2 * n_cast + 1:]
    cast_refs = tuple(zip(cast_in, cast_out))
    step = pl.program_id(0)

    @pl.when(step == 0)
    def _init():
        state_s[...] = jnp.zeros_like(state_s)
        kprev_s[...] = jnp.zeros_like(kprev_s)
        vprev_s[...] = jnp.zeros_like(vprev_s)
        zbuf_s[0:CONV_HIST, :] = jnp.zeros((CONV_HIST, zbuf_s.shape[1]), F32)
        xnew_s[...] = jnp.zeros_like(xnew_s)

    for src, dst in cast_refs:
        dst[...] = src[...].astype(BF16)

    xmid_s[...] = xnew_s[...]

    def param_row(r):
        return rows_ref[0, r:r + 1, :]

    h2 = _modulated_rms_norm(xmid_s[...], param_row(ROW_GFFN) * (1.0 + param_row(ROW_SCALE2)), param_row(ROW_SHIFT2))
    slabs = iter(range(act_s.shape[1] // FFN_COL))

    def fill(n):
        for _ in range(n):
            j = next(slabs, None)
            if j is not None:
                _ffn_slab(j, h2, wgu_ref, act_s)

    xnew_s[...] = _mixer_tile(
        dims, layer, step == 0, x_ref[...], sinks_ref, rows_ref, small_ref, tables_ref, win_ref, cos_ref, sin_ref,
        wout_ref, proj_s, mix_s, state_s, kprev_s, vprev_s, zbuf_s, fill)
    fill(act_s.shape[1] // FFN_COL)
    o_ref[...] = xmid_s[...] + param_row(ROW_GATE2) * _dot(act_s[...], wdown_ref[0])


def _fixed_block(block, index):
    return pl.BlockSpec(block, lambda i: index, pipeline_mode=pl.Buffered(1))


def _whole(arr):
    return _fixed_block(arr.shape, (0,) * arr.ndim)


def _layer_slice(arr, layer):
    return _fixed_block((1,) + arr.shape[1:], (layer,) + (0,) * (arr.ndim - 1))


def _cast_chunk_rows(rows, n_steps):
    return next(c for c in range(BF16_SUBLANES, rows + 1, BF16_SUBLANES)
                if rows % c == 0 and rows // c <= n_steps)


def _layer(layer, x, sinks, row_params, small, tables, weights, cos_t, sin_t, next_f32, dims):
    seq, d = x.shape
    ret_w, swa_w, kv_w, conv_c = dims
    w_in, w_out, w_gu, w_down = weights
    in_w = w_in.shape[2]
    hidden = w_down.shape[1]
    n_pairs = ret_w // LANES
    tile = ROW_TILE
    n_tiles = seq // tile

    def mixer_rows(i):
        return (jnp.minimum(i, n_tiles - 1), 0)

    def ffn_rows(i):
        return (jnp.maximum(i - 1, 0), 0)

    def chunk_rows(i, which, last):
        return (which, jnp.minimum(i, last), 0)

    cast_in_specs, cast_out_specs, cast_out_shapes = [], [], []
    for w in next_f32:
        _, rows, cols = w.shape
        chunk = _cast_chunk_rows(rows, n_tiles)
        last = rows // chunk - 1
        cast_in_specs.append(
            pl.BlockSpec((1, chunk, cols), functools.partial(chunk_rows, which=layer + 1, last=last)))
        cast_out_specs.append(
            pl.BlockSpec((1, chunk, cols), functools.partial(chunk_rows, which=0, last=last)))
        cast_out_shapes.append(jax.ShapeDtypeStruct((1, rows, cols), BF16))

    outs = pl.pallas_call(
        functools.partial(_layer_kernel, dims, layer, len(next_f32)),
        grid=(n_tiles + 1,),
        in_specs=[
            pl.BlockSpec(memory_space=pltpu.SMEM),
            pl.BlockSpec((tile, d), mixer_rows),
            _layer_slice(row_params, layer),
            _layer_slice(small, layer),
            _whole(tables),
            _whole(w_in),
            pl.BlockSpec((tile, HEAD_DIM // 2), mixer_rows),
            pl.BlockSpec((tile, HEAD_DIM // 2), mixer_rows),
            _whole(w_out),
            _whole(w_gu),
            _whole(w_down),
        ] + cast_in_specs,
        out_specs=[pl.BlockSpec((tile, d), ffn_rows)] + cast_out_specs,
        out_shape=[jax.ShapeDtypeStruct((seq, d), F32)] + cast_out_shapes,
        scratch_shapes=[
            pltpu.VMEM((tile, in_w), F32),
            pltpu.VMEM((tile, d), BF16),
            pltpu.VMEM((n_pairs, LANES, LANES), F32),
            pltpu.VMEM((SWA_KV_HEADS, CHUNK, LANES), BF16),
            pltpu.VMEM((SWA_KV_HEADS, CHUNK, LANES), BF16),
            pltpu.VMEM((tile + CONV_HIST, conv_c), F32),
            pltpu.VMEM((tile, d), F32),
            pltpu.VMEM((tile, d), F32),
            pltpu.VMEM((tile, hidden), BF16),
        ],
        compiler_params=pltpu.CompilerParams(
            dimension_semantics=("arbitrary",), vmem_limit_bytes=VMEM_LIMIT_BYTES),
        name="layer",
    )(sinks, x, row_params, small, tables, w_in, cos_t, sin_t, w_out, w_gu, w_down, *next_f32)
    return outs[0], tuple(outs[1:])


def _rope_tables(seq):
    half = HEAD_DIM // 2
    per_row = LANES // half
    inv = 1.0 / (ROPE_THETA ** (jnp.arange(0, HEAD_DIM, 2, dtype=F32) / HEAD_DIM))
    lane = jnp.arange(LANES)
    pos = (jnp.arange(seq // per_row)[:, None] * per_row + (lane // half)[None, :]).astype(F32)
    ang = pos * inv[lane % half][None, :]
    cos, sin = lax.optimization_barrier((jnp.cos(ang), jnp.sin(ang)))
    return cos.reshape(seq, half), sin.reshape(seq, half)


def _retention_tables(ret_heads):
    log_gamma = jnp.log(1.0 - 2.0 ** (-5.0 - jnp.arange(ret_heads, dtype=F32)))
    idx = jnp.arange(CHUNK, dtype=F32)
    rel = idx[:, None] - idx[None, :]
    intra = jnp.where(rel >= 0, jnp.exp(log_gamma[:, None, None] * jnp.maximum(rel, 0.0)), 0.0)
    k_decay = jnp.exp(log_gamma[:, None] * (CHUNK - 1.0 - idx)[None, :])
    q_decay = jnp.exp(log_gamma[:, None] * (idx + 1.0)[None, :])
    chunk_decay = jnp.exp(log_gamma * CHUNK)
    n_pairs = ret_heads // 2

    def lanes(per_head):
        wide = jnp.repeat(per_head[:, :, None], HEAD_DIM, axis=2)
        return jnp.concatenate([wide[0::2], wide[1::2]], axis=2)

    dpair = jnp.concatenate([intra[0::2], intra[1::2]], axis=2)
    sdec = jnp.broadcast_to(
        jnp.repeat(chunk_decay.reshape(n_pairs, 2), HEAD_DIM, axis=1)[:, :, None], (n_pairs, LANES, LANES))
    return jnp.concatenate([dpair, lanes(q_decay), lanes(k_decay), sdec], axis=2)


@jax.jit
def kernel(x, c, w_ada, b_ada, g_mix, w_in, conv_w, conv_b, q_norm, k_norm, sinks, ret_gn, w_out, g_ffn,
           w_gu, w_down):
    batch, seq, d = x.shape
    depth = w_ada.shape[0]
    assert batch == 1 and seq % ROW_TILE == 0
    ret_w = ret_gn.shape[1]
    swa_w = sinks.shape[1] * HEAD_DIM
    conv_c = conv_w.shape[2]
    kv_w = SWA_KV_HEADS * HEAD_DIM
    assert w_in.shape[2] == 4 * ret_w + swa_w + 2 * kv_w + 3 * conv_c
    assert ret_w % LANES == 0 and swa_w % LANES == 0 and kv_w == LANES and conv_c % LANES == 0
    assert w_down.shape[1] % FFN_COL == 0 and w_ada.shape[2] % ADA_COL == 0
    dims = (ret_w, swa_w, kv_w, conv_c)

    mod = _ada_modulation(c, w_ada, b_ada).reshape(depth, 6, d)
    cos_t, sin_t = _rope_tables(seq)
    tables = _retention_tables(ret_w // HEAD_DIM)
    reps = LANES // HEAD_DIM
    row_params = jnp.concatenate([mod, g_mix[:, None, :], g_ffn[:, None, :]], axis=1)
    small_w = max(ret_w, 2 * LANES, conv_c)

    def widen(a):
        return jnp.pad(a, ((0, 0), (0, 0), (0, small_w - a.shape[2])))

    qk = jnp.concatenate([jnp.tile(q_norm, (1, reps)), jnp.tile(k_norm, (1, reps))], axis=1)[:, None, :]
    small = jnp.concatenate([widen(ret_gn[:, None, :]), widen(qk), widen(conv_w), widen(conv_b[:, None, :])], axis=1)
    small = jnp.pad(small, ((0, 0), (0, 8 - small.shape[1]), (0, 0)))

    f32_weights = (w_in, w_out, w_gu, w_down)
    weights = tuple(w[:1].astype(BF16) for w in f32_weights)
    xs = x[0]
    for l in range(depth):
        xs, weights = _layer(l, xs, sinks, row_params, small, tables, weights, cos_t, sin_t,
                             f32_weights if l + 1 < depth else (), dims)
    return xs[None]
```

```python
import functools

import jax
import jax.numpy as jnp
from jax import lax
from jax.experimental import pallas as pl
from jax.experimental.pallas import tpu as pltpu

HEAD_DIM = 64
LANES = 128
BF16_SUBLANES = 16
SWA_KV_HEADS = 2
CONV_K = 3
CONV_HIST = 8
CHUNK = 128
ROPE_THETA = 10000.0
EPS = 1e-6
NEG_INF = -1e30
ROW_TILE = 256
FFN_COL = 256
FILL_BEFORE_PROJ = 3
ADA_COL = 2048
VMEM_LIMIT_BYTES = 60 * 1024 * 1024

ROW_SHIFT1, ROW_SCALE1, ROW_GATE1, ROW_SHIFT2, ROW_SCALE2, ROW_GATE2, ROW_GMIX, ROW_GFFN = range(8)
SMALL_GN, SMALL_QK, SMALL_CONV = 0, 1, 2

F32 = jnp.float32
BF16 = jnp.bfloat16


def _dot(a, b):
    return jnp.dot(a, b, preferred_element_type=F32)


def _dot_nt(a, b):
    return lax.dot_general(a, b, (((1,), (1,)), ((), ())), preferred_element_type=F32)


def _dot_tn(a, b):
    return lax.dot_general(a, b, (((0,), (0,)), ((), ())), preferred_element_type=F32)


def _lane_lo(shape):
    return lax.broadcasted_iota(jnp.int32, shape, 1) % LANES < HEAD_DIM


def _head_sum(a, lo):
    s_lo = jnp.sum(jnp.where(lo, a, 0.0), axis=-1, keepdims=True)
    s_hi = jnp.sum(jnp.where(lo, 0.0, a), axis=-1, keepdims=True)
    return jnp.where(lo, s_lo, s_hi)


def _rope(a, cos, sin_signed, first_half):
    width = a.shape[-1]
    fwd = pltpu.roll(a, width - HEAD_DIM // 2, 1)
    bwd = pltpu.roll(a, HEAD_DIM // 2, 1)
    return a * cos + jnp.where(first_half, fwd, bwd) * sin_signed


def _silu(a):
    return a * jax.nn.sigmoid(a)


def _modulated_rms_norm(x, gain_scale, shift):
    ms = jnp.mean(x * x, axis=-1, keepdims=True)
    return (x * lax.rsqrt(ms + EPS) * gain_scale + shift).astype(BF16)


def _ada_kernel(c_ref, w_ref, b_ref, o_ref):
    s = _silu(c_ref[...])
    for j in range(o_ref.shape[2] // LANES):
        cols = pl.ds(j * LANES, LANES)
        o_ref[0, :, cols] = jnp.sum(w_ref[0, :, cols] * s, axis=0, keepdims=True) + b_ref[0, :, cols]


def _ada_modulation(c, w_ada, b_ada):
    depth, d, e = w_ada.shape
    c_lanes = jnp.broadcast_to(c.reshape(d, 1), (d, LANES))
    return pl.pallas_call(
        _ada_kernel,
        grid=(depth, e // ADA_COL),
        in_specs=[
            pl.BlockSpec((d, LANES), lambda l, j: (0, 0)),
            pl.BlockSpec((1, d, ADA_COL), lambda l, j: (l, 0, j)),
            pl.BlockSpec((1, 1, ADA_COL), lambda l, j: (l, 0, j)),
        ],
        out_specs=pl.BlockSpec((1, 1, ADA_COL), lambda l, j: (l, 0, j)),
        out_shape=jax.ShapeDtypeStruct((depth, 1, e), F32),
        compiler_params=pltpu.CompilerParams(
            dimension_semantics=("arbitrary", "arbitrary"), vmem_limit_bytes=VMEM_LIMIT_BYTES),
        name="ada_modulation",
    )(c_lanes, w_ada, b_ada.reshape(depth, 1, e))


def _mixer_tile(dims, layer, first_tile, x, sinks_ref, rows_ref, small_ref, tables_ref, win_ref, cos_ref, sin_ref,
                wout_ref, proj_s, mix_s, state_s, kprev_s, vprev_s, zbuf_s, fill, fill_per_point):
    ret_w, swa_w, kv_w, conv_c = dims
    n_pairs = ret_w // LANES
    n_qcols = swa_w // LANES
    group = (swa_w // HEAD_DIM) // SWA_KV_HEADS
    tile = x.shape[0]

    off_rq, off_rk, off_rv, off_rg = 0, ret_w, 2 * ret_w, 3 * ret_w
    off_aq = 4 * ret_w
    off_ak = off_aq + swa_w
    off_av = off_ak + kv_w
    off_cb = off_av + kv_w
    off_cc = off_cb + conv_c
    off_cu = off_cc + conv_c

    def param_row(r):
        return rows_ref[0, r:r + 1, :]

    h = _modulated_rms_norm(x, param_row(ROW_GMIX) * (1.0 + param_row(ROW_SCALE1)), param_row(ROW_SHIFT1))
    fill(FILL_BEFORE_PROJ)
    proj_s[...] = _dot(h, win_ref[0])

    sq = (CHUNK, LANES)
    lo = _lane_lo(sq)
    first_half = lax.broadcasted_iota(jnp.int32, sq, 1) % HEAD_DIM < HEAD_DIM // 2
    row = lax.broadcasted_iota(jnp.int32, sq, 0)
    col = lax.broadcasted_iota(jnp.int32, sq, 1)
    block_diag = (row < HEAD_DIM) == (col < HEAD_DIM)
    qi = lax.broadcasted_iota(jnp.int32, (CHUNK, 2 * CHUNK), 0)
    kj = lax.broadcasted_iota(jnp.int32, (CHUNK, 2 * CHUNK), 1)
    band = (kj > qi) & (kj <= qi + CHUNK)
    zero_b = jnp.zeros(sq, BF16)
    reps = LANES // (HEAD_DIM // 2)
    sin_sign = jnp.where(first_half, -1.0, 1.0)
    ones_b = jnp.ones((2 * CHUNK, LANES), BF16)

    def blk(off, c, width=LANES):
        return (pl.ds(c * CHUNK, CHUNK), pl.ds(off, width))

    def table(p, off, width=LANES):
        return tables_ref[p, :, pl.ds(off, width)]

    q_gain = small_ref[0, SMALL_QK:SMALL_QK + 1, 0:LANES]
    k_gain = small_ref[0, SMALL_QK:SMALL_QK + 1, LANES:2 * LANES]

    states = [state_s[p] for p in range(n_pairs)]
    k_prev = [kprev_s[grp] for grp in range(SWA_KV_HEADS)]
    v_prev = [vprev_s[grp] for grp in range(SWA_KV_HEADS)]

    for c in range(tile // CHUNK):
        rows = pl.ds(c * CHUNK, CHUNK)
        cos = jnp.concatenate([cos_ref[rows, :]] * reps, axis=1)
        sin = jnp.concatenate([sin_ref[rows, :]] * reps, axis=1) * sin_sign

        qs, ks, vs, scores = [], [], [], []
        for p in range(n_pairs):
            q = _rope(proj_s[blk(off_rq + p * LANES, c)], cos, sin, first_half)
            k = _rope(proj_s[blk(off_rk + p * LANES, c)], cos, sin, first_half) * (HEAD_DIM ** -0.5)
            v = proj_s[blk(off_rv + p * LANES, c)].astype(BF16)
            kb = k.astype(BF16)
            k_bd = jnp.concatenate([jnp.where(lo, kb, zero_b), jnp.where(lo, zero_b, kb)], axis=0)
            scores.append(_dot_nt(q.astype(BF16), k_bd) * table(p, 0, 2 * CHUNK))
            qs.append(q)
            ks.append(k)
            vs.append(v)
        fill(fill_per_point)
        for p in range(n_pairs):
            q, k, v = qs[p], ks[p], vs[p]
            g = proj_s[blk(off_rg + p * LANES, c)]
            v_bd = jnp.concatenate([jnp.where(lo, v, zero_b), jnp.where(lo, zero_b, v)], axis=0)
            inner = _dot(scores[p].astype(BF16), v_bd)
            cross = _dot((q * table(p, 2 * CHUNK)).astype(BF16), states[p].astype(BF16))
            kv = _dot_tn((k * table(p, 2 * CHUNK + LANES)).astype(BF16), v)
            states[p] = states[p] * table(p, 2 * CHUNK + 2 * LANES) + jnp.where(block_diag, kv, 0.0)
            o = inner + cross
            mu = _head_sum(o, lo) * (1.0 / HEAD_DIM)
            d = o - mu
            var = _head_sum(d * d, lo) * (1.0 / HEAD_DIM)
            o = d * lax.rsqrt(var + EPS) * small_ref[0, SMALL_GN:SMALL_GN + 1, pl.ds(p * LANES, LANES)]
            mix_s[blk(p * LANES, c)] = (o * _silu(g)).astype(BF16)

        def head_norm(a, gain):
            ms = _head_sum(a * a, lo) * (1.0 / HEAD_DIM)
            return a * lax.rsqrt(ms + EPS) * gain

        k = _rope(head_norm(proj_s[blk(off_ak, c)], k_gain), cos, sin, first_half)
        v = proj_s[blk(off_av, c)]
        k_sw = pltpu.roll(k, HEAD_DIM, 1)
        v_sw = pltpu.roll(v, HEAD_DIM, 1)
        k_cur = [jnp.where(lo, k, k_sw).astype(BF16), jnp.where(lo, k_sw, k).astype(BF16)]
        v_cur = [jnp.where(lo, v, v_sw).astype(BF16), jnp.where(lo, v_sw, v).astype(BF16)]
        qcols = [
            _rope(head_norm(proj_s[blk(off_aq + j * LANES, c)], q_gain), cos, sin, first_half)
            * (HEAD_DIM ** -0.5)
            for j in range(n_qcols)
        ]
        if c == 0:
            first_key = jnp.where(first_tile, CHUNK, 0)
            mask = band & (kj >= first_key)
        else:
            mask = band
        heads_out, swa_scores = [], []
        for grp in range(SWA_KV_HEADS):
            heads = [grp * group + j for j in range(group)]
            q_st = jnp.concatenate(
                [jnp.where(lo if hd % 2 == 0 else ~lo, qcols[hd // 2], 0.0).astype(BF16) for hd in heads],
                axis=0)
            k_win = jnp.concatenate([k_prev[grp], k_cur[grp]], axis=0)
            swa_scores.append(_dot_nt(q_st, k_win))
        fill(fill_per_point)
        for grp in range(SWA_KV_HEADS):
            heads = [grp * group + j for j in range(group)]
            v_win = jnp.concatenate(
                [jnp.concatenate([v_prev[grp], v_cur[grp]], axis=0), ones_b], axis=1)
            sc = swa_scores[grp]
            probs, maxes = [], []
            for j, hd in enumerate(heads):
                s_h = jnp.where(mask, sc[j * CHUNK:(j + 1) * CHUNK], NEG_INF)
                m = jnp.maximum(jnp.max(s_h, axis=-1, keepdims=True), sinks_ref[layer, hd])
                probs.append(jnp.exp(s_h - m).astype(BF16))
                maxes.append(m)
            ov = _dot(jnp.concatenate(probs, axis=0), v_win)
            for j, hd in enumerate(heads):
                o_h = ov[j * CHUNK:(j + 1) * CHUNK]
                denom = o_h[:, LANES:] + jnp.exp(sinks_ref[layer, hd] - maxes[j])
                heads_out.append(o_h[:, :LANES] / denom)
        for j in range(n_qcols):
            mix_s[blk(ret_w + j * LANES, c)] = jnp.where(lo, heads_out[2 * j], heads_out[2 * j + 1]).astype(BF16)
        k_prev, v_prev = k_cur, v_cur

        z = proj_s[blk(off_cc, c, conv_c)] * proj_s[blk(off_cu, c, conv_c)]
        zbuf_s[pl.ds(CONV_HIST + c * CHUNK, CHUNK), :] = z
        def conv_row(r):
            return small_ref[0, SMALL_CONV + r:SMALL_CONV + r + 1, 0:conv_c]

        y = conv_row(CONV_K)
        for back in range(CONV_K):
            z_back = z if back == 0 else zbuf_s[pl.ds(CONV_HIST - back + c * CHUNK, CHUNK), :]
            y = z_back * conv_row(CONV_K - 1 - back) + y
        mix_s[blk(ret_w + swa_w, c, conv_c)] = (proj_s[blk(off_cb, c, conv_c)] * y).astype(BF16)

    for p in range(n_pairs):
        state_s[p] = states[p]
    for grp in range(SWA_KV_HEADS):
        kprev_s[grp] = k_prev[grp]
        vprev_s[grp] = v_prev[grp]
    zbuf_s[0:CONV_HIST, :] = zbuf_s[tile:tile + CONV_HIST, :]
    return x + param_row(ROW_GATE1) * _dot(mix_s[...], wout_ref[0])


def _ffn_slab(j, h, wgu_ref, act_s):
    hidden = act_s.shape[1]
    cols = pl.ds(j * FFN_COL, FFN_COL)
    gt = _dot(h, wgu_ref[0, :, cols])
    up = _dot(h, wgu_ref[0, :, pl.ds(hidden + j * FFN_COL, FFN_COL)])
    act_s[:, cols] = (_silu(gt) * up).astype(BF16)


N_LAYER_INPUTS = 11


def _layer_kernel(dims, layer, n_cast, *refs):
    (sinks_ref, x_ref, rows_ref, small_ref, tables_ref, win_ref, cos_ref, sin_ref, wout_ref, wgu_ref,
     wdown_ref) = refs[:N_LAYER_INPUTS]
    cast_in = refs[N_LAYER_INPUTS:N_LAYER_INPUTS + n_cast]
    o_ref = refs[N_LAYER_INPUTS + n_cast]
    cast_out = refs[N_LAYER_INPUTS + n_cast + 1:N_LAYER_INPUTS + 2 * n_cast + 1]
    (proj_s, mix_s, state_s, kprev_s, vprev_s, zbuf_s, xmid_s, xnew_s, act_s) = refs[N_LAYER_INPUTS + 2 * n_cast + 1:]
    cast_refs = tuple(zip(cast_in, cast_out))
    step = pl.program_id(0)

    @pl.when(step == 0)
    def _init():
        state_s[...] = jnp.zeros_like(state_s)
        kprev_s[...] = jnp.zeros_like(kprev_s)
        vprev_s[...] = jnp.zeros_like(vprev_s)
        zbuf_s[0:CONV_HIST, :] = jnp.zeros((CONV_HIST, zbuf_s.shape[1]), F32)
        xnew_s[...] = jnp.zeros_like(xnew_s)

    for src, dst in cast_refs:
        dst[...] = src[...].astype(BF16)

    xmid_s[...] = xnew_s[...]

    def param_row(r):
        return rows_ref[0, r:r + 1, :]

    h2 = _modulated_rms_norm(xmid_s[...], param_row(ROW_GFFN) * (1.0 + param_row(ROW_SCALE2)), param_row(ROW_SHIFT2))
    n_slabs = act_s.shape[1] // FFN_COL
    slabs = iter(range(n_slabs))
    fill_per_point = max(1, (n_slabs - FILL_BEFORE_PROJ) // (2 * (x_ref.shape[0] // CHUNK)))

    def fill(n):
        for _ in range(n):
            j = next(slabs, None)
            if j is not None:
                _ffn_slab(j, h2, wgu_ref, act_s)

    xnew_s[...] = _mixer_tile(
        dims, layer, step == 0, x_ref[...], sinks_ref, rows_ref, small_ref, tables_ref, win_ref, cos_ref, sin_ref,
        wout_ref, proj_s, mix_s, state_s, kprev_s, vprev_s, zbuf_s, fill, fill_per_point)
    fill(act_s.shape[1] // FFN_COL)
    o_ref[...] = xmid_s[...] + param_row(ROW_GATE2) * _dot(act_s[...], wdown_ref[0])


def _fixed_block(block, index):
    return pl.BlockSpec(block, lambda i: index, pipeline_mode=pl.Buffered(1))


def _whole(arr):
    return _fixed_block(arr.shape, (0,) * arr.ndim)


def _layer_slice(arr, layer):
    return _fixed_block((1,) + arr.shape[1:], (layer,) + (0,) * (arr.ndim - 1))


def _cast_chunk_rows(rows, n_steps):
    return next(c for c in range(BF16_SUBLANES, rows + 1, BF16_SUBLANES)
                if rows % c == 0 and rows // c <= n_steps)


def _layer(layer, x, sinks, row_params, small, tables, weights, cos_t, sin_t, next_f32, dims):
    seq, d = x.shape
    ret_w, swa_w, kv_w, conv_c = dims
    w_in, w_out, w_gu, w_down = weights
    in_w = w_in.shape[2]
    hidden = w_down.shape[1]
    n_pairs = ret_w // LANES
    tile = ROW_TILE
    n_tiles = seq // tile

    def mixer_rows(i):
        return (jnp.minimum(i, n_tiles - 1), 0)

    def ffn_rows(i):
        return (jnp.maximum(i - 1, 0), 0)

    def chunk_rows(i, which, last):
        return (which, jnp.minimum(i, last), 0)

    cast_in_specs, cast_out_specs, cast_out_shapes = [], [], []
    for w in next_f32:
        _, rows, cols = w.shape
        chunk = _cast_chunk_rows(rows, n_tiles)
        last = rows // chunk - 1
        cast_in_specs.append(
            pl.BlockSpec((1, chunk, cols), functools.partial(chunk_rows, which=layer + 1, last=last)))
        cast_out_specs.append(
            pl.BlockSpec((1, chunk, cols), functools.partial(chunk_rows, which=0, last=last)))
        cast_out_shapes.append(jax.ShapeDtypeStruct((1, rows, cols), BF16))

    outs = pl.pallas_call(
        functools.partial(_layer_kernel, dims, layer, len(next_f32)),
        grid=(n_tiles + 1,),
        in_specs=[
            pl.BlockSpec(memory_space=pltpu.SMEM),
            pl.BlockSpec((tile, d), mixer_rows),
            _layer_slice(row_params, layer),
            _layer_slice(small, layer),
            _whole(tables),
            _whole(w_in),
            pl.BlockSpec((tile, HEAD_DIM // 2), mixer_rows),
            pl.BlockSpec((tile, HEAD_DIM // 2), mixer_rows),
            _whole(w_out),
            _whole(w_gu),
            _whole(w_down),
        ] + cast_in_specs,
        out_specs=[pl.BlockSpec((tile, d), ffn_rows)] + cast_out_specs,
        out_shape=[jax.ShapeDtypeStruct((seq, d), F32)] + cast_out_shapes,
        scratch_shapes=[
            pltpu.VMEM((tile, in_w), F32),
            pltpu.VMEM((tile, d), BF16),
            pltpu.VMEM((n_pairs, LANES, LANES), F32),
            pltpu.VMEM((SWA_KV_HEADS, CHUNK, LANES), BF16),
            pltpu.VMEM((SWA_KV_HEADS, CHUNK, LANES), BF16),
            pltpu.VMEM((tile + CONV_HIST, conv_c), F32),
            pltpu.VMEM((tile, d), F32),
            pltpu.VMEM((tile, d), F32),
            pltpu.VMEM((tile, hidden), BF16),
        ],
        compiler_params=pltpu.CompilerParams(
            dimension_semantics=("arbitrary",), vmem_limit_bytes=VMEM_LIMIT_BYTES),
        name="layer",
    )(sinks, x, row_params, small, tables, w_in, cos_t, sin_t, w_out, w_gu, w_down, *next_f32)
    return outs[0], tuple(outs[1:])


def _rope_tables(seq):
    half = HEAD_DIM // 2
    per_row = LANES // half
    inv = 1.0 / (ROPE_THETA ** (jnp.arange(0, HEAD_DIM, 2, dtype=F32) / HEAD_DIM))
    lane = jnp.arange(LANES)
    pos = (jnp.arange(seq // per_row)[:, None] * per_row + (lane // half)[None, :]).astype(F32)
    ang = pos * inv[lane % half][None, :]
    cos, sin = lax.optimization_barrier((jnp.cos(ang), jnp.sin(ang)))
    return cos.reshape(seq, half), sin.reshape(seq, half)


def _retention_tables(ret_heads):
    log_gamma = jnp.log(1.0 - 2.0 ** (-5.0 - jnp.arange(ret_heads, dtype=F32)))
    idx = jnp.arange(CHUNK, dtype=F32)
    rel = idx[:, None] - idx[None, :]
    intra = jnp.where(rel >= 0, jnp.exp(log_gamma[:, None, None] * jnp.maximum(rel, 0.0)), 0.0)
    k_decay = jnp.exp(log_gamma[:, None] * (CHUNK - 1.0 - idx)[None, :])
    q_decay = jnp.exp(log_gamma[:, None] * (idx + 1.0)[None, :])
    chunk_decay = jnp.exp(log_gamma * CHUNK)
    n_pairs = ret_heads // 2

    def lanes(per_head):
        wide = jnp.repeat(per_head[:, :, None], HEAD_DIM, axis=2)
        return jnp.concatenate([wide[0::2], wide[1::2]], axis=2)

    dpair = jnp.concatenate([intra[0::2], intra[1::2]], axis=2)
    sdec = jnp.broadcast_to(
        jnp.repeat(chunk_decay.reshape(n_pairs, 2), HEAD_DIM, axis=1)[:, :, None], (n_pairs, LANES, LANES))
    return jnp.concatenate([dpair, lanes(q_decay), lanes(k_decay), sdec], axis=2)


@jax.jit
def kernel(x, c, w_ada, b_ada, g_mix, w_in, conv_w, conv_b, q_norm, k_norm, sinks, ret_gn, w_out, g_ffn,
           w_gu, w_down):
    batch, seq, d = x.shape
    depth = w_ada.shape[0]
    assert batch == 1 and seq % ROW_TILE == 0
    ret_w = ret_gn.shape[1]
    swa_w = sinks.shape[1] * HEAD_DIM
    conv_c = conv_w.shape[2]
    kv_w = SWA_KV_HEADS * HEAD_DIM
    assert w_in.shape[2] == 4 * ret_w + swa_w + 2 * kv_w + 3 * conv_c
    assert ret_w % LANES == 0 and swa_w % LANES == 0 and kv_w == LANES and conv_c % LANES == 0
    assert w_down.shape[1] % FFN_COL == 0 and w_ada.shape[2] % ADA_COL == 0
    dims = (ret_w, swa_w, kv_w, conv_c)

    mod = _ada_modulation(c, w_ada, b_ada).reshape(depth, 6, d)
    cos_t, sin_t = _rope_tables(seq)
    tables = _retention_tables(ret_w // HEAD_DIM)
    reps = LANES // HEAD_DIM
    row_params = jnp.concatenate([mod, g_mix[:, None, :], g_ffn[:, None, :]], axis=1)
    small_w = max(ret_w, 2 * LANES, conv_c)

    def widen(a):
        return jnp.pad(a, ((0, 0), (0, 0), (0, small_w - a.shape[2])))

    qk = jnp.concatenate([jnp.tile(q_norm, (1, reps)), jnp.tile(k_norm, (1, reps))], axis=1)[:, None, :]
    small = jnp.concatenate([widen(ret_gn[:, None, :]), widen(qk), widen(conv_w), widen(conv_b[:, None, :])], axis=1)
    small = jnp.pad(small, ((0, 0), (0, 8 - small.shape[1]), (0, 0)))

    f32_weights = (w_in, w_out, w_gu, w_down)
    weights = tuple(w[:1].astype(BF16) for w in f32_weights)
    xs = x[0]
    for l in range(depth):
        xs, weights = _layer(l, xs, sinks, row_params, small, tables, weights, cos_t, sin_t,
                             f32_weights if l + 1 < depth else (), dims)
    return xs[None]
```

```python
import functools

import jax
import jax.numpy as jnp
from jax import lax
from jax.experimental import pallas as pl
from jax.experimental.pallas import tpu as pltpu

HEAD_DIM = 64
LANES = 128
BF16_SUBLANES = 16
SWA_KV_HEADS = 2
CONV_K = 3
CONV_HIST = 8
CHUNK = 128
ROPE_THETA = 10000.0
EPS = 1e-6
NEG_INF = -1e30
ROW_TILE = 256
FFN_COL = 256
FILL_AFTER_PROJ = 3
ADA_COL = 2048
VMEM_LIMIT_BYTES = 60 * 1024 * 1024

ROW_SHIFT1, ROW_SCALE1, ROW_GATE1, ROW_SHIFT2, ROW_SCALE2, ROW_GATE2, ROW_GMIX, ROW_GFFN = range(8)
SMALL_GN, SMALL_QK, SMALL_CONV = 0, 1, 2

F32 = jnp.float32
BF16 = jnp.bfloat16


def _dot(a, b):
    return jnp.dot(a, b, preferred_element_type=F32)


def _dot_nt(a, b):
    return lax.dot_general(a, b, (((1,), (1,)), ((), ())), preferred_element_type=F32)


def _dot_tn(a, b):
    return lax.dot_general(a, b, (((0,), (0,)), ((), ())), preferred_element_type=F32)


def _lane_lo(shape):
    return lax.broadcasted_iota(jnp.int32, shape, 1) % LANES < HEAD_DIM


def _head_sum(a, lo):
    s_lo = jnp.sum(jnp.where(lo, a, 0.0), axis=-1, keepdims=True)
    s_hi = jnp.sum(jnp.where(lo, 0.0, a), axis=-1, keepdims=True)
    return jnp.where(lo, s_lo, s_hi)


def _rope(a, cos, sin_signed, first_half):
    width = a.shape[-1]
    fwd = pltpu.roll(a, width - HEAD_DIM // 2, 1)
    bwd = pltpu.roll(a, HEAD_DIM // 2, 1)
    return a * cos + jnp.where(first_half, fwd, bwd) * sin_signed


def _silu(a):
    return a * jax.nn.sigmoid(a)


def _modulated_rms_norm(x, gain_scale, shift):
    ms = jnp.mean(x * x, axis=-1, keepdims=True)
    return (x * lax.rsqrt(ms + EPS) * gain_scale + shift).astype(BF16)


def _ada_kernel(c_ref, w_ref, b_ref, o_ref):
    s = _silu(c_ref[...])
    for j in range(o_ref.shape[2] // LANES):
        cols = pl.ds(j * LANES, LANES)
        o_ref[0, :, cols] = jnp.sum(w_ref[0, :, cols] * s, axis=0, keepdims=True) + b_ref[0, :, cols]


def _ada_modulation(c, w_ada, b_ada):
    depth, d, e = w_ada.shape
    c_lanes = jnp.broadcast_to(c.reshape(d, 1), (d, LANES))
    return pl.pallas_call(
        _ada_kernel,
        grid=(depth, e // ADA_COL),
        in_specs=[
            pl.BlockSpec((d, LANES), lambda l, j: (0, 0)),
            pl.BlockSpec((1, d, ADA_COL), lambda l, j: (l, 0, j)),
            pl.BlockSpec((1, 1, ADA_COL), lambda l, j: (l, 0, j)),
        ],
        out_specs=pl.BlockSpec((1, 1, ADA_COL), lambda l, j: (l, 0, j)),
        out_shape=jax.ShapeDtypeStruct((depth, 1, e), F32),
        compiler_params=pltpu.CompilerParams(
            dimension_semantics=("arbitrary", "arbitrary"), vmem_limit_bytes=VMEM_LIMIT_BYTES),
        name="ada_modulation",
    )(c_lanes, w_ada, b_ada.reshape(depth, 1, e))


def _mixer_tile(dims, layer, first_tile, x, sinks_ref, rows_ref, small_ref, tables_ref, win_ref, cos_ref, sin_ref,
                proj_s, mix_s, state_s, kprev_s, vprev_s, zbuf_s, fill, fill_per_point):
    ret_w, swa_w, kv_w, conv_c = dims
    n_pairs = ret_w // LANES
    n_qcols = swa_w // LANES
    group = (swa_w // HEAD_DIM) // SWA_KV_HEADS
    tile = x.shape[0]

    off_rq, off_rk, off_rv, off_rg = 0, ret_w, 2 * ret_w, 3 * ret_w
    off_aq = 4 * ret_w
    off_ak = off_aq + swa_w
    off_av = off_ak + kv_w
    off_cb = off_av + kv_w
    off_cc = off_cb + conv_c
    off_cu = off_cc + conv_c

    def param_row(r):
        return rows_ref[0, r:r + 1, :]

    h = _modulated_rms_norm(x, param_row(ROW_GMIX) * (1.0 + param_row(ROW_SCALE1)), param_row(ROW_SHIFT1))
    proj_s[...] = _dot(h, win_ref[0])
    fill(FILL_AFTER_PROJ)

    sq = (CHUNK, LANES)
    lo = _lane_lo(sq)
    first_half = lax.broadcasted_iota(jnp.int32, sq, 1) % HEAD_DIM < HEAD_DIM // 2
    row = lax.broadcasted_iota(jnp.int32, sq, 0)
    col = lax.broadcasted_iota(jnp.int32, sq, 1)
    block_diag = (row < HEAD_DIM) == (col < HEAD_DIM)
    qi = lax.broadcasted_iota(jnp.int32, (CHUNK, 2 * CHUNK), 0)
    kj = lax.broadcasted_iota(jnp.int32, (CHUNK, 2 * CHUNK), 1)
    band = (kj > qi) & (kj <= qi + CHUNK)
    zero_b = jnp.zeros(sq, BF16)
    reps = LANES // (HEAD_DIM // 2)
    sin_sign = jnp.where(first_half, -1.0, 1.0)
    ones_b = jnp.ones((2 * CHUNK, LANES), BF16)

    def blk(off, c, width=LANES):
        return (pl.ds(c * CHUNK, CHUNK), pl.ds(off, width))

    def table(p, off, width=LANES):
        return tables_ref[p, :, pl.ds(off, width)]

    q_gain = small_ref[0, SMALL_QK:SMALL_QK + 1, 0:LANES]
    k_gain = small_ref[0, SMALL_QK:SMALL_QK + 1, LANES:2 * LANES]

    states = [state_s[p] for p in range(n_pairs)]
    k_prev = [kprev_s[grp] for grp in range(SWA_KV_HEADS)]
    v_prev = [vprev_s[grp] for grp in range(SWA_KV_HEADS)]

    for c in range(tile // CHUNK):
        rows = pl.ds(c * CHUNK, CHUNK)
        cos = jnp.concatenate([cos_ref[rows, :]] * reps, axis=1)
        sin = jnp.concatenate([sin_ref[rows, :]] * reps, axis=1) * sin_sign

        qs, ks, vs, scores = [], [], [], []
        for p in range(n_pairs):
            q = _rope(proj_s[blk(off_rq + p * LANES, c)], cos, sin, first_half)
            k = _rope(proj_s[blk(off_rk + p * LANES, c)], cos, sin, first_half) * (HEAD_DIM ** -0.5)
            v = proj_s[blk(off_rv + p * LANES, c)].astype(BF16)
            kb = k.astype(BF16)
            k_bd = jnp.concatenate([jnp.where(lo, kb, zero_b), jnp.where(lo, zero_b, kb)], axis=0)
            scores.append(_dot_nt(q.astype(BF16), k_bd) * table(p, 0, 2 * CHUNK))
            qs.append(q)
            ks.append(k)
            vs.append(v)
        fill(fill_per_point)
        for p in range(n_pairs):
            q, k, v = qs[p], ks[p], vs[p]
            g = proj_s[blk(off_rg + p * LANES, c)]
            v_bd = jnp.concatenate([jnp.where(lo, v, zero_b), jnp.where(lo, zero_b, v)], axis=0)
            inner = _dot(scores[p].astype(BF16), v_bd)
            cross = _dot((q * table(p, 2 * CHUNK)).astype(BF16), states[p].astype(BF16))
            kv = _dot_tn((k * table(p, 2 * CHUNK + LANES)).astype(BF16), v)
            states[p] = states[p] * table(p, 2 * CHUNK + 2 * LANES) + jnp.where(block_diag, kv, 0.0)
            o = inner + cross
            mu = _head_sum(o, lo) * (1.0 / HEAD_DIM)
            d = o - mu
            var = _head_sum(d * d, lo) * (1.0 / HEAD_DIM)
            o = d * lax.rsqrt(var + EPS) * small_ref[0, SMALL_GN:SMALL_GN + 1, pl.ds(p * LANES, LANES)]
            mix_s[blk(p * LANES, c)] = (o * _silu(g)).astype(BF16)

        def head_norm(a, gain):
            ms = _head_sum(a * a, lo) * (1.0 / HEAD_DIM)
            return a * lax.rsqrt(ms + EPS) * gain

        k = _rope(head_norm(proj_s[blk(off_ak, c)], k_gain), cos, sin, first_half)
        v = proj_s[blk(off_av, c)]
        k_sw = pltpu.roll(k, HEAD_DIM, 1)
        v_sw = pltpu.roll(v, HEAD_DIM, 1)
        k_cur = [jnp.where(lo, k, k_sw).astype(BF16), jnp.where(lo, k_sw, k).astype(BF16)]
        v_cur = [jnp.where(lo, v, v_sw).astype(BF16), jnp.where(lo, v_sw, v).astype(BF16)]
        qcols = [
            _rope(head_norm(proj_s[blk(off_aq + j * LANES, c)], q_gain), cos, sin, first_half)
            * (HEAD_DIM ** -0.5)
            for j in range(n_qcols)
        ]
        if c == 0:
            first_key = jnp.where(first_tile, CHUNK, 0)
            mask = band & (kj >= first_key)
        else:
            mask = band
        heads_out, swa_scores = [], []
        for grp in range(SWA_KV_HEADS):
            heads = [grp * group + j for j in range(group)]
            q_st = jnp.concatenate(
                [jnp.where(lo if hd % 2 == 0 else ~lo, qcols[hd // 2], 0.0).astype(BF16) for hd in heads],
                axis=0)
            k_win = jnp.concatenate([k_prev[grp], k_cur[grp]], axis=0)
            swa_scores.append(_dot_nt(q_st, k_win))
        fill(fill_per_point)
        for grp in range(SWA_KV_HEADS):
            heads = [grp * group + j for j in range(group)]
            v_win = jnp.concatenate(
                [jnp.concatenate([v_prev[grp], v_cur[grp]], axis=0), ones_b], axis=1)
            sc = swa_scores[grp]
            probs, maxes = [], []
            for j, hd in enumerate(heads):
                s_h = jnp.where(mask, sc[j * CHUNK:(j + 1) * CHUNK], NEG_INF)
                m = jnp.maximum(jnp.max(s_h, axis=-1, keepdims=True), sinks_ref[layer, hd])
                probs.append(jnp.exp(s_h - m).astype(BF16))
                maxes.append(m)
            ov = _dot(jnp.concatenate(probs, axis=0), v_win)
            for j, hd in enumerate(heads):
                o_h = ov[j * CHUNK:(j + 1) * CHUNK]
                denom = o_h[:, LANES:] + jnp.exp(sinks_ref[layer, hd] - maxes[j])
                heads_out.append(o_h[:, :LANES] / denom)
        for j in range(n_qcols):
            mix_s[blk(ret_w + j * LANES, c)] = jnp.where(lo, heads_out[2 * j], heads_out[2 * j + 1]).astype(BF16)
        k_prev, v_prev = k_cur, v_cur

        z = proj_s[blk(off_cc, c, conv_c)] * proj_s[blk(off_cu, c, conv_c)]
        zbuf_s[pl.ds(CONV_HIST + c * CHUNK, CHUNK), :] = z
        def conv_row(r):
            return small_ref[0, SMALL_CONV + r:SMALL_CONV + r + 1, 0:conv_c]

        y = conv_row(CONV_K)
        for back in range(CONV_K):
            z_back = z if back == 0 else zbuf_s[pl.ds(CONV_HIST - back + c * CHUNK, CHUNK), :]
            y = z_back * conv_row(CONV_K - 1 - back) + y
        mix_s[blk(ret_w + swa_w, c, conv_c)] = (proj_s[blk(off_cb, c, conv_c)] * y).astype(BF16)

    for p in range(n_pairs):
        state_s[p] = states[p]
    for grp in range(SWA_KV_HEADS):
        kprev_s[grp] = k_prev[grp]
        vprev_s[grp] = v_prev[grp]
    zbuf_s[0:CONV_HIST, :] = zbuf_s[tile:tile + CONV_HIST, :]


def _ffn_slab(j, h, wgu_ref, act_s):
    hidden = act_s.shape[1]
    cols = pl.ds(j * FFN_COL, FFN_COL)
    gt = _dot(h, wgu_ref[0, :, cols])
    up = _dot(h, wgu_ref[0, :, pl.ds(hidden + j * FFN_COL, FFN_COL)])
    act_s[:, cols] = (_silu(gt) * up).astype(BF16)


N_LAYER_INPUTS = 11


def _layer_kernel(dims, layer, n_cast, *refs):
    (sinks_ref, x_ref, rows_ref, small_ref, tables_ref, win_ref, cos_ref, sin_ref, wout_ref, wgu_ref,
     wdown_ref) = refs[:N_LAYER_INPUTS]
    cast_in = refs[N_LAYER_INPUTS:N_LAYER_INPUTS + n_cast]
    o_ref = refs[N_LAYER_INPUTS + n_cast]
    cast_out = refs[N_LAYER_INPUTS + n_cast + 1:N_LAYER_INPUTS + 2 * n_cast + 1]
    (proj_s, mix_s, state_s, kprev_s, vprev_s, zbuf_s, xmid_s, xprev_s, mixprev_s, act_s) = refs[N_LAYER_INPUTS + 2 * n_cast + 1:]
    cast_refs = tuple(zip(cast_in, cast_out))
    step = pl.program_id(0)

    @pl.when(step == 0)
    def _init():
        state_s[...] = jnp.zeros_like(state_s)
        kprev_s[...] = jnp.zeros_like(kprev_s)
        vprev_s[...] = jnp.zeros_like(vprev_s)
        zbuf_s[0:CONV_HIST, :] = jnp.zeros((CONV_HIST, zbuf_s.shape[1]), F32)
        xprev_s[...] = jnp.zeros_like(xprev_s)
        mixprev_s[...] = jnp.zeros_like(mixprev_s)

    for src, dst in cast_refs:
        dst[...] = src[...].astype(BF16)

    def param_row(r):
        return rows_ref[0, r:r + 1, :]

    xmid_s[...] = xprev_s[...] + param_row(ROW_GATE1) * _dot(mixprev_s[...], wout_ref[0])

    h2 = _modulated_rms_norm(xmid_s[...], param_row(ROW_GFFN) * (1.0 + param_row(ROW_SCALE2)), param_row(ROW_SHIFT2))
    n_slabs = act_s.shape[1] // FFN_COL
    slabs = iter(range(n_slabs))
    fill_per_point = max(1, (n_slabs - FILL_AFTER_PROJ) // (2 * (x_ref.shape[0] // CHUNK)))

    def fill(n):
        for _ in range(n):
            j = next(slabs, None)
            if j is not None:
                _ffn_slab(j, h2, wgu_ref, act_s)

    _mixer_tile(
        dims, layer, step == 0, x_ref[...], sinks_ref, rows_ref, small_ref, tables_ref, win_ref, cos_ref, sin_ref,
        proj_s, mix_s, state_s, kprev_s, vprev_s, zbuf_s, fill, fill_per_point)
    fill(act_s.shape[1] // FFN_COL)
    o_ref[...] = xmid_s[...] + param_row(ROW_GATE2) * _dot(act_s[...], wdown_ref[0])
    xprev_s[...] = x_ref[...]
    mixprev_s[...] = mix_s[...]


def _fixed_block(block, index):
    return pl.BlockSpec(block, lambda i: index, pipeline_mode=pl.Buffered(1))


def _whole(arr):
    return _fixed_block(arr.shape, (0,) * arr.ndim)


def _layer_slice(arr, layer):
    return _fixed_block((1,) + arr.shape[1:], (layer,) + (0,) * (arr.ndim - 1))


def _cast_chunk_rows(rows, n_steps):
    return next(c for c in range(BF16_SUBLANES, rows + 1, BF16_SUBLANES)
                if rows % c == 0 and rows // c <= n_steps)


def _layer(layer, x, sinks, row_params, small, tables, weights, cos_t, sin_t, next_f32, dims):
    seq, d = x.shape
    ret_w, swa_w, kv_w, conv_c = dims
    w_in, w_out, w_gu, w_down = weights
    in_w = w_in.shape[2]
    hidden = w_down.shape[1]
    n_pairs = ret_w // LANES
    tile = ROW_TILE
    n_tiles = seq // tile

    def mixer_rows(i):
        return (jnp.minimum(i, n_tiles - 1), 0)

    def ffn_rows(i):
        return (jnp.maximum(i - 1, 0), 0)

    def chunk_rows(i, which, last):
        return (which, jnp.minimum(i, last), 0)

    cast_in_specs, cast_out_specs, cast_out_shapes = [], [], []
    for w in next_f32:
        _, rows, cols = w.shape
        chunk = _cast_chunk_rows(rows, n_tiles)
        last = rows // chunk - 1
        cast_in_specs.append(
            pl.BlockSpec((1, chunk, cols), functools.partial(chunk_rows, which=layer + 1, last=last)))
        cast_out_specs.append(
            pl.BlockSpec((1, chunk, cols), functools.partial(chunk_rows, which=0, last=last)))
        cast_out_shapes.append(jax.ShapeDtypeStruct((1, rows, cols), BF16))

    outs = pl.pallas_call(
        functools.partial(_layer_kernel, dims, layer, len(next_f32)),
        grid=(n_tiles + 1,),
        in_specs=[
            pl.BlockSpec(memory_space=pltpu.SMEM),
            pl.BlockSpec((tile, d), mixer_rows),
            _layer_slice(row_params, layer),
            _layer_slice(small, layer),
            _whole(tables),
            _whole(w_in),
            pl.BlockSpec((tile, HEAD_DIM // 2), mixer_rows),
            pl.BlockSpec((tile, HEAD_DIM // 2), mixer_rows),
            _whole(w_out),
            _whole(w_gu),
            _whole(w_down),
        ] + cast_in_specs,
        out_specs=[pl.BlockSpec((tile, d), ffn_rows)] + cast_out_specs,
        out_shape=[jax.ShapeDtypeStruct((seq, d), F32)] + cast_out_shapes,
        scratch_shapes=[
            pltpu.VMEM((tile, in_w), F32),
            pltpu.VMEM((tile, d), BF16),
            pltpu.VMEM((n_pairs, LANES, LANES), F32),
            pltpu.VMEM((SWA_KV_HEADS, CHUNK, LANES), BF16),
            pltpu.VMEM((SWA_KV_HEADS, CHUNK, LANES), BF16),
            pltpu.VMEM((tile + CONV_HIST, conv_c), F32),
            pltpu.VMEM((tile, d), F32),
            pltpu.VMEM((tile, d), F32),
            pltpu.VMEM((tile, d), BF16),
            pltpu.VMEM((tile, hidden), BF16),
        ],
        compiler_params=pltpu.CompilerParams(
            dimension_semantics=("arbitrary",), vmem_limit_bytes=VMEM_LIMIT_BYTES),
        name="layer",
    )(sinks, x, row_params, small, tables, w_in, cos_t, sin_t, w_out, w_gu, w_down, *next_f32)
    return outs[0], tuple(outs[1:])


def _rope_tables(seq):
    half = HEAD_DIM // 2
    per_row = LANES // half
    inv = 1.0 / (ROPE_THETA ** (jnp.arange(0, HEAD_DIM, 2, dtype=F32) / HEAD_DIM))
    lane = jnp.arange(LANES)
    pos = (jnp.arange(seq // per_row)[:, None] * per_row + (lane // half)[None, :]).astype(F32)
    ang = pos * inv[lane % half][None, :]
    cos, sin = lax.optimization_barrier((jnp.cos(ang), jnp.sin(ang)))
    return cos.reshape(seq, half), sin.reshape(seq, half)


def _retention_tables(ret_heads):
    log_gamma = jnp.log(1.0 - 2.0 ** (-5.0 - jnp.arange(ret_heads, dtype=F32)))
    idx = jnp.arange(CHUNK, dtype=F32)
    rel = idx[:, None] - idx[None, :]
    intra = jnp.where(rel >= 0, jnp.exp(log_gamma[:, None, None] * jnp.maximum(rel, 0.0)), 0.0)
    k_decay = jnp.exp(log_gamma[:, None] * (CHUNK - 1.0 - idx)[None, :])
    q_decay = jnp.exp(log_gamma[:, None] * (idx + 1.0)[None, :])
    chunk_decay = jnp.exp(log_gamma * CHUNK)
    n_pairs = ret_heads // 2

    def lanes(per_head):
        wide = jnp.repeat(per_head[:, :, None], HEAD_DIM, axis=2)
        return jnp.concatenate([wide[0::2], wide[1::2]], axis=2)

    dpair = jnp.concatenate([intra[0::2], intra[1::2]], axis=2)
    sdec = jnp.broadcast_to(
        jnp.repeat(chunk_decay.reshape(n_pairs, 2), HEAD_DIM, axis=1)[:, :, None], (n_pairs, LANES, LANES))
    return jnp.concatenate([dpair, lanes(q_decay), lanes(k_decay), sdec], axis=2)


@jax.jit
def kernel(x, c, w_ada, b_ada, g_mix, w_in, conv_w, conv_b, q_norm, k_norm, sinks, ret_gn, w_out, g_ffn,
           w_gu, w_down):
    batch, seq, d = x.shape
    depth = w_ada.shape[0]
    assert batch == 1 and seq % ROW_TILE == 0
    ret_w = ret_gn.shape[1]
    swa_w = sinks.shape[1] * HEAD_DIM
    conv_c = conv_w.shape[2]
    kv_w = SWA_KV_HEADS * HEAD_DIM
    assert w_in.shape[2] == 4 * ret_w + swa_w + 2 * kv_w + 3 * conv_c
    assert ret_w % LANES == 0 and swa_w % LANES == 0 and kv_w == LANES and conv_c % LANES == 0
    assert w_down.shape[1] % FFN_COL == 0 and w_ada.shape[2] % ADA_COL == 0
    dims = (ret_w, swa_w, kv_w, conv_c)

    mod = _ada_modulation(c, w_ada, b_ada).reshape(depth, 6, d)
    cos_t, sin_t = _rope_tables(seq)
    tables = _retention_tables(ret_w // HEAD_DIM)
    reps = LANES // HEAD_DIM
    row_params = jnp.concatenate([mod, g_mix[:, None, :], g_ffn[:, None, :]], axis=1)
    small_w = max(ret_w, 2 * LANES, conv_c)

    def widen(a):
        return jnp.pad(a, ((0, 0), (0, 0), (0, small_w - a.shape[2])))

    qk = jnp.concatenate([jnp.tile(q_norm, (1, reps)), jnp.tile(k_norm, (1, reps))], axis=1)[:, None, :]
    small = jnp.concatenate([widen(ret_gn[:, None, :]), widen(qk), widen(conv_w), widen(conv_b[:, None, :])], axis=1)
    small = jnp.pad(small, ((0, 0), (0, 8 - small.shape[1]), (0, 0)))

    f32_weights = (w_in, w_out, w_gu, w_down)
    weights = tuple(w[:1].astype(BF16) for w in f32_weights)
    xs = x[0]
    for l in range(depth):
        xs, weights = _layer(l, xs, sinks, row_params, small, tables, weights, cos_t, sin_t,
                             f32_weights if l + 1 < depth else (), dims)
    return xs[None]
```
